```python
import jax, jax.numpy as jnp
from jax import lax
import numpy as np

D_MODEL = 1024
BATCH = 32
SEQ = 2048
DEPTH = 2

HEAD_DIM = 64
N_SB_HEADS = 8
D_SB = N_SB_HEADS * HEAD_DIM
N_SGU_GROUPS = 8
D_SGU = N_SGU_GROUPS * HEAD_DIM
N_GROUPS = N_SB_HEADS + N_SGU_GROUPS
D_MIX = D_SB + D_SGU
D_IN = 3 * D_SB + 2 * D_SGU
BLOCK = 128
CHUNK = 128
D_FF = 2816
N_EXPERTS = 8
TOP_K = 2
D_FF_EXPERT = 1408
N_DENSE = (DEPTH + 1) // 2
N_MOE = DEPTH // 2
EPS = 1e-6

kernel_name = "hybrid_stickbreak_sgu_moe_block"


def rms_norm(x, g):
    xf = x.astype(jnp.float32)
    var = jnp.mean(xf * xf, axis=-1, keepdims=True)
    return (xf * lax.rsqrt(var + EPS)).astype(x.dtype) * g


def group_rms_norm(x, g):
    return rms_norm(x, g.reshape(x.shape[-2], x.shape[-1]))


def stick_breaking_attention(q, k, v):
    seq = q.shape[1]
    scale = HEAD_DIM ** -0.5
    outs = []
    for i in range(seq // BLOCK):
        q0 = i * BLOCK
        end = q0 + BLOCK
        qb = q[:, q0:end]
        kb = k[:, :end]
        vb = v[:, :end]
        z = jnp.einsum('bqhd,bkhd->bhqk', qb.astype(jnp.float32), kb.astype(jnp.float32)) * scale
        qpos = q0 + jnp.arange(BLOCK)[:, None]
        kpos = jnp.arange(end)[None, :]
        causal = kpos < qpos
        log_one_minus = jnp.where(causal, -jax.nn.softplus(z), 0.0)
        between = lax.cumsum(log_one_minus, axis=3, reverse=True) - log_one_minus
        a = jnp.where(causal, jnp.exp(jax.nn.log_sigmoid(z) + between), 0.0)
        outs.append(jnp.einsum('bhqk,bkhd->bqhd', a.astype(v.dtype), vb))
    return jnp.concatenate(outs, axis=1)


def chunked_spatial_gating(u, gate, w_s, b_s, g_gate):
    bsz, seq, _ = u.shape
    u = jax.nn.gelu(u).reshape(bsz, seq, N_SGU_GROUPS, HEAD_DIM)
    gate = jax.nn.gelu(gate).reshape(bsz, seq, N_SGU_GROUPS, HEAD_DIM)
    gate = group_rms_norm(gate, g_gate)
    gate = gate.reshape(bsz, seq // CHUNK, CHUNK, N_SGU_GROUPS, HEAD_DIM)
    w_causal = jnp.tril(w_s)
    mixed = jnp.einsum('gts,bcsgd->bctgd', w_causal, gate) + b_s.T[None, None, :, :, None]
    return u * mixed.reshape(bsz, seq, N_SGU_GROUPS, HEAD_DIM)


def swiglu(x, wg, wu, wd):
    return (jax.nn.silu(x @ wg) * (x @ wu)) @ wd


def moe_swiglu(h, router_w, wg, wu, wd):
    bsz, seq, d = h.shape
    xt = h.reshape(-1, d)
    logits = (xt @ router_w).astype(jnp.float32)
    top_logits, top_idx = lax.top_k(logits, TOP_K)
    top_w = jax.nn.softmax(top_logits, axis=-1)
    gates = jnp.sum(jax.nn.one_hot(top_idx, N_EXPERTS, dtype=jnp.float32) * top_w[..., None], axis=1)
    gates = gates.astype(h.dtype)
    out = jnp.zeros_like(xt)
    for e in range(N_EXPERTS):
        out = out + gates[:, e:e + 1] * swiglu(xt, wg[e], wu[e], wd[e])
    return out.reshape(bsz, seq, d)


def setup_inputs(seed: int = 0) -> dict:
    key = jax.random.key(seed)
    ks = jax.random.split(key, 17)
    f32 = jnp.float32
    nrm = lambda k, shape, s: jax.random.normal(k, shape, f32) * s
    gain = lambda k, shape: 1.0 + 0.05 * jax.random.normal(k, shape, f32)
    return {
        "x": jax.random.normal(ks[0], (BATCH, SEQ, D_MODEL), f32),
        "w_in": nrm(ks[1], (DEPTH, D_MODEL, D_IN), D_MODEL ** -0.5),
        "w_out": nrm(ks[2], (DEPTH, D_MIX, D_MODEL), D_MIX ** -0.5),
        "g_mix": gain(ks[3], (DEPTH, D_MODEL)),
        "g_ffn": gain(ks[4], (DEPTH, D_MODEL)),
        "g_sgu": gain(ks[5], (DEPTH, D_SGU)),
        "sgu_w": nrm(ks[6], (DEPTH, N_SGU_GROUPS, CHUNK, CHUNK), CHUNK ** -0.5),
        "sgu_b": 1.0 + 0.1 * jax.random.normal(ks[7], (DEPTH, N_SGU_GROUPS, CHUNK), f32),
        "g_out": gain(ks[8], (DEPTH, D_MIX)),
        "ffn_w_gate": nrm(ks[9], (N_DENSE, D_MODEL, D_FF), D_MODEL ** -0.5),
        "ffn_w_up": nrm(ks[10], (N_DENSE, D_MODEL, D_FF), D_MODEL ** -0.5),
        "ffn_w_down": nrm(ks[11], (N_DENSE, D_FF, D_MODEL), D_FF ** -0.5),
        "router_w": nrm(ks[12], (N_MOE, D_MODEL, N_EXPERTS), D_MODEL ** -0.5),
        "moe_w_gate": nrm(ks[13], (N_MOE, N_EXPERTS, D_MODEL, D_FF_EXPERT), D_MODEL ** -0.5),
        "moe_w_up": nrm(ks[14], (N_MOE, N_EXPERTS, D_MODEL, D_FF_EXPERT), D_MODEL ** -0.5),
        "moe_w_down": nrm(ks[15], (N_MOE, N_EXPERTS, D_FF_EXPERT, D_MODEL), D_FF_EXPERT ** -0.5),
        "g_final": gain(ks[16], (D_MODEL,)),
    }


def reference(x, w_in, w_out, g_mix, g_ffn, g_sgu, sgu_w, sgu_b, g_out,
              ffn_w_gate, ffn_w_up, ffn_w_down, router_w,
              moe_w_gate, moe_w_up, moe_w_down, g_final):
    bsz, seq, _ = x.shape
    for l in range(DEPTH):
        h = rms_norm(x, g_mix[l])
        proj = h @ w_in[l]
        q, k, v, u, gate = jnp.split(
            proj, [D_SB, 2 * D_SB, 3 * D_SB, 3 * D_SB + D_SGU], axis=-1)
        q = q.reshape(bsz, seq, N_SB_HEADS, HEAD_DIM)
        k = k.reshape(bsz, seq, N_SB_HEADS, HEAD_DIM)
        v = v.reshape(bsz, seq, N_SB_HEADS, HEAD_DIM)
        y_sb = stick_breaking_attention(q, k, v)
        y_sgu = chunked_spatial_gating(u, gate, sgu_w[l], sgu_b[l], g_sgu[l])
        y = jnp.concatenate([y_sb, y_sgu], axis=2)
        y = group_rms_norm(y, g_out[l]).reshape(bsz, seq, D_MIX)
        x = x + y @ w_out[l]
        h2 = rms_norm(x, g_ffn[l])
        if l % 2 == 0:
            i = l // 2
            x = x + swiglu(h2, ffn_w_gate[i], ffn_w_up[i], ffn_w_down[i])
        else:
            i = l // 2
            x = x + moe_swiglu(h2, router_w[i], moe_w_gate[i], moe_w_up[i], moe_w_down[i])
    return rms_norm(x, g_final)
```

```python
import functools

import jax
import jax.numpy as jnp
from jax import lax
from jax.experimental import pallas as pl
from jax.experimental.pallas import tpu as pltpu

HEAD_DIM = 64
BLK = 128
LANES = 128
EPS = 1e-6
FF_CHUNK = 256
VMEM_LIMIT = 56 * 1024 * 1024

F32 = jnp.float32
BF16 = jnp.bfloat16


def _params(*sem):
    return pltpu.CompilerParams(dimension_semantics=sem, vmem_limit_bytes=VMEM_LIMIT)


def _split_bf16(a):
    hi = a.astype(BF16)
    lo = (a - hi.astype(F32)).astype(BF16)
    return hi, lo


def _same_head_mean_matrix():
    r = lax.broadcasted_iota(jnp.int32, (LANES, LANES), 0)
    c = lax.broadcasted_iota(jnp.int32, (LANES, LANES), 1)
    return jnp.where((r < HEAD_DIM) == (c < HEAD_DIM), 1.0 / HEAD_DIM, 0.0).astype(BF16)


def _head_rms_norm(y, mean_mat, gain):
    hi, lo = _split_bf16(y * y)
    ms = (jnp.dot(hi, mean_mat, preferred_element_type=F32)
          + jnp.dot(lo, mean_mat, preferred_element_type=F32))
    return y * lax.rsqrt(ms + EPS) * gain


def _in_proj_kernel(x_ref, g_ref, cs_ref, w_ref, o_ref, *, n_chunk):
    x = x_ref[...]
    ms = jnp.mean(x * x, axis=-1, keepdims=True)
    h = (x * lax.rsqrt(ms + EPS) * g_ref[...]).astype(BF16)
    for c in range(o_ref.shape[1] // n_chunk):
        sl = slice(c * n_chunk, (c + 1) * n_chunk)
        acc = jnp.dot(h, w_ref[:, sl], preferred_element_type=F32)
        o_ref[:, sl] = (acc * cs_ref[:, sl]).astype(o_ref.dtype)


def _in_proj(x, g, col_scale, w, tm):
    n, d = x.shape
    d_in = w.shape[1]
    return pl.pallas_call(
        functools.partial(_in_proj_kernel, n_chunk=512),
        out_shape=jax.ShapeDtypeStruct((n, d_in), BF16),
        grid=(n // tm,),
        in_specs=[
            pl.BlockSpec((tm, d), lambda i: (i, 0)),
            pl.BlockSpec((1, d), lambda i: (0, 0)),
            pl.BlockSpec((1, d_in), lambda i: (0, 0)),
            pl.BlockSpec((d, d_in), lambda i: (0, 0), pipeline_mode=pl.Buffered(1)),
        ],
        out_specs=pl.BlockSpec((tm, d_in), lambda i: (i, 0)),
        compiler_params=_params("parallel"),
        name="in_proj",
    )(x, g, col_scale, w)


def _attn_kernel(q_ref, k_ref, v_ref, g_ref, o_ref, q2_ref, v0_ref, v1_ref, *, seq):
    nq = seq // BLK
    row = lax.broadcasted_iota(jnp.int32, (BLK, BLK), 0)
    col = lax.broadcasted_iota(jnp.int32, (BLK, BLK), 1)
    upper = jnp.where(row > col, 1.0, 0.0).astype(BF16)
    upper2 = jnp.concatenate([upper, upper], axis=0)
    causal = jnp.concatenate([col < row, col < row], axis=0)
    mean_mat = _same_head_mean_matrix()
    lane = lax.broadcasted_iota(jnp.int32, (1, LANES), 1)
    m0 = jnp.where(lane < HEAD_DIM, 1.0, 0.0).astype(BF16)
    m1 = jnp.where(lane < HEAD_DIM, 0.0, 1.0).astype(BF16)
    gain = g_ref[...]

    v0_ref[...] = v_ref[...] * m0
    v1_ref[...] = v_ref[...] * m1
    q2_ref[:, :BLK, :] = (q_ref[...] * m0).reshape(nq, BLK, LANES)
    q2_ref[:, BLK:, :] = (q_ref[...] * m1).reshape(nq, BLK, LANES)

    def block(q2, j, c, acc, mask):
        ks = pl.multiple_of(j * BLK, BLK)
        z = lax.dot_general(q2, k_ref[pl.ds(ks, BLK), :], (((1,), (1,)), ((), ())),
                            preferred_element_type=F32)
        sp = jnp.maximum(z, 0.0) + jnp.log(1.0 + jnp.exp(-jnp.abs(z)))
        log_beta = z - sp
        log_om = -sp
        if mask is not None:
            log_om = jnp.where(mask, log_om, 0.0)
        hi, lo = _split_bf16(log_om)
        between = jnp.dot(jnp.concatenate([hi, lo], axis=1), upper2,
                          preferred_element_type=F32) + c
        a = jnp.exp(log_beta + between)
        if mask is not None:
            a = jnp.where(mask, a, 0.0)
        a = a.astype(BF16)
        acc = (acc
               + jnp.dot(a[:BLK], v0_ref[pl.ds(ks, BLK), :], preferred_element_type=F32)
               + jnp.dot(a[BLK:], v1_ref[pl.ds(ks, BLK), :], preferred_element_type=F32))
        c = c + jnp.sum(log_om, axis=-1, keepdims=True)
        return c, acc

    def q_block(i, carry):
        q2 = q2_ref[i]
        c, acc = block(q2, i, jnp.zeros((2 * BLK, 1), F32), jnp.zeros((BLK, LANES), F32), causal)

        def k_block(jj, ca):
            return block(q2, i - 1 - jj, ca[0], ca[1], None)

        c, acc = lax.fori_loop(0, i, k_block, (c, acc))
        qs = pl.multiple_of(i * BLK, BLK)
        o_ref[pl.ds(qs, BLK), :] = _head_rms_norm(acc, mean_mat, gain).astype(o_ref.dtype)
        return carry

    lax.fori_loop(0, nq, q_block, 0)


def _attention(proj, g_sb, batch, seq, d_sb):
    n = proj.shape[0]
    pairs = d_sb // LANES
    blk = lambda off: pl.BlockSpec((seq, LANES), lambda b, p: (b, off + p))
    return pl.pallas_call(
        functools.partial(_attn_kernel, seq=seq),
        out_shape=jax.ShapeDtypeStruct((n, d_sb), BF16),
        grid=(batch, pairs),
        in_specs=[blk(0), blk(pairs), blk(2 * pairs),
                  pl.BlockSpec((1, LANES), lambda b, p: (0, p))],
        out_specs=pl.BlockSpec((seq, LANES), lambda b, p: (b, p)),
        scratch_shapes=[pltpu.VMEM((seq // BLK, 2 * BLK, LANES), BF16),
                        pltpu.VMEM((seq, LANES), BF16),
                        pltpu.VMEM((seq, LANES), BF16)],
        compiler_params=_params("parallel", "parallel"),
        name="stickbreak_attn",
    )(proj, proj, proj, g_sb)


def _sgu_kernel(u_ref, gt_ref, w_ref, b_ref, gs_ref, go_ref, o_ref):
    tm, d_sgu = o_ref.shape
    row = lax.broadcasted_iota(jnp.int32, (BLK, BLK), 0)
    col = lax.broadcasted_iota(jnp.int32, (BLK, BLK), 1)
    tril = col <= row
    head0 = col < HEAD_DIM
    mean_mat = _same_head_mean_matrix()
    for p in range(d_sgu // LANES):
        ls = slice(p * LANES, (p + 1) * LANES)
        w0 = jnp.where(tril, w_ref[2 * p], 0.0).astype(BF16)
        w1 = jnp.where(tril, w_ref[2 * p + 1], 0.0).astype(BF16)
        for c in range(tm // BLK):
            rs = slice(c * BLK, (c + 1) * BLK)
            gate = jax.nn.gelu(gt_ref[rs, ls].astype(F32))
            gate = _head_rms_norm(gate, mean_mat, gs_ref[:, ls]).astype(BF16)
            mixed = jnp.where(head0,
                              jnp.dot(w0, gate, preferred_element_type=F32),
                              jnp.dot(w1, gate, preferred_element_type=F32)) + b_ref[:, ls]
            y = jax.nn.gelu(u_ref[rs, ls].astype(F32)) * mixed
            o_ref[rs, ls] = _head_rms_norm(y, mean_mat, go_ref[:, ls]).astype(o_ref.dtype)


def _sgu(proj, w_s, bias, g_gate, g_o, d_sb, d_sgu, tm):
    n = proj.shape[0]
    u_blk = 3 * d_sb // d_sgu
    return pl.pallas_call(
        _sgu_kernel,
        out_shape=jax.ShapeDtypeStruct((n, d_sgu), BF16),
        grid=(n // tm,),
        in_specs=[
            pl.BlockSpec((tm, d_sgu), lambda i: (i, u_blk)),
            pl.BlockSpec((tm, d_sgu), lambda i: (i, u_blk + 1)),
            pl.BlockSpec(w_s.shape, lambda i: (0, 0, 0)),
            pl.BlockSpec(bias.shape, lambda i: (0, 0)),
            pl.BlockSpec((1, d_sgu), lambda i: (0, 0)),
            pl.BlockSpec((1, d_sgu), lambda i: (0, 0)),
        ],
        out_specs=pl.BlockSpec((tm, d_sgu), lambda i: (i, 0)),
        compiler_params=_params("parallel"),
        name="spatial_gating",
    )(proj, proj, w_s, bias, g_gate, g_o)


def _top2_gates(logits, n_experts):
    lane = lax.broadcasted_iota(jnp.int32, logits.shape, 1).astype(F32)
    neg = jnp.float32(-jnp.inf)
    lg = jnp.where(lane < n_experts, logits, neg)
    m1 = jnp.max(lg, axis=-1, keepdims=True)
    i1 = jnp.min(jnp.where(lg == m1, lane, float(LANES)), axis=-1, keepdims=True)
    lg2 = jnp.where(lane == i1, neg, lg)
    m2 = jnp.max(lg2, axis=-1, keepdims=True)
    i2 = jnp.min(jnp.where(lg2 == m2, lane, float(LANES)), axis=-1, keepdims=True)
    e2 = jnp.exp(m2 - m1)
    w1 = 1.0 / (1.0 + e2)
    w2 = e2 / (1.0 + e2)
    return jnp.where(lane == i1, w1, 0.0) + jnp.where(lane == i2, w2, 0.0)


def _out_proj_kernel(*refs, n_experts):
    if n_experts:
        ysb_ref, ysgu_ref, x_ref, w_ref, g_ref, rw_ref, xo_ref, h_ref, gates_ref = refs
    else:
        ysb_ref, ysgu_ref, x_ref, w_ref, g_ref, xo_ref, h_ref = refs
    y = jnp.concatenate([ysb_ref[...], ysgu_ref[...]], axis=1)
    xn = x_ref[...] + jnp.dot(y, w_ref[...], preferred_element_type=F32)
    xo_ref[...] = xn
    ms = jnp.mean(xn * xn, axis=-1, keepdims=True)
    h = xn * lax.rsqrt(ms + EPS) * g_ref[...]
    h_ref[...] = h.astype(h_ref.dtype)
    if n_experts:
        h_hi, h_lo = _split_bf16(h)
        w_hi, w_lo = _split_bf16(rw_ref[...])
        logits = (jnp.dot(h_hi, w_hi, preferred_element_type=F32)
                  + jnp.dot(h_lo, w_hi, preferred_element_type=F32)
                  + jnp.dot(h_hi, w_lo, preferred_element_type=F32))
        gates_ref[...] = _top2_gates(logits, n_experts)


def _out_proj(ysb, ysgu, x, w, g, router_w, n_experts, tm):
    n, d = x.shape
    d_half = ysb.shape[1]
    row_blk = lambda width: pl.BlockSpec((tm, width), lambda i: (i, 0))
    const = lambda shape: pl.BlockSpec(shape, lambda i: (0, 0))
    in_specs = [row_blk(d_half), row_blk(ysgu.shape[1]), row_blk(d), const(w.shape), const((1, d))]
    out_shape = [jax.ShapeDtypeStruct((n, d), F32), jax.ShapeDtypeStruct((n, d), BF16)]
    out_specs = [row_blk(d), row_blk(d)]
    args = [ysb, ysgu, x, w, g]
    if n_experts:
        in_specs.append(const(router_w.shape))
        out_shape.append(jax.ShapeDtypeStruct((n, LANES), F32))
        out_specs.append(row_blk(LANES))
        args.append(router_w)
    return pl.pallas_call(
        functools.partial(_out_proj_kernel, n_experts=n_experts),
        out_shape=out_shape,
        grid=(n // tm,),
        in_specs=in_specs,
        out_specs=out_specs,
        compiler_params=_params("parallel"),
        name="out_proj_router" if n_experts else "out_proj",
    )(*args)


def _final_norm(x, gain):
    ms = jnp.mean(x * x, axis=-1, keepdims=True)
    return x * lax.rsqrt(ms + EPS) * gain


def _ffn_kernel(x_ref, h_ref, wg_ref, wu_ref, wd_ref, gf_ref, o_ref, *, final):
    h = h_ref[...]
    o_ref[...] = x_ref[...]

    def chunk(c, carry):
        g = jnp.dot(h, wg_ref[c], preferred_element_type=F32)
        u = jnp.dot(h, wu_ref[c], preferred_element_type=F32)
        act = (jax.nn.silu(g) * u).astype(BF16)
        o_ref[...] += jnp.dot(act, wd_ref[c], preferred_element_type=F32)
        return carry

    lax.fori_loop(0, wg_ref.shape[0], chunk, 0)
    if final:
        o_ref[...] = _final_norm(o_ref[...], gf_ref[...])


def _ffn(x, h, wg, wu, wd, g_final, final, tm):
    n, d = x.shape
    const3 = lambda a: pl.BlockSpec(a.shape, lambda i: (0, 0, 0), pipeline_mode=pl.Buffered(1))
    return pl.pallas_call(
        functools.partial(_ffn_kernel, final=final),
        out_shape=jax.ShapeDtypeStruct((n, d), F32),
        grid=(n // tm,),
        in_specs=[
            pl.BlockSpec((tm, d), lambda i: (i, 0)),
            pl.BlockSpec((tm, d), lambda i: (i, 0)),
            const3(wg), const3(wu), const3(wd),
            pl.BlockSpec((1, d), lambda i: (0, 0)),
        ],
        out_specs=pl.BlockSpec((tm, d), lambda i: (i, 0)),
        compiler_params=_params("parallel"),
        name="dense_swiglu",
    )(x, h, wg, wu, wd, g_final)


def _ff_chunks(d_ff):
    starts = list(range(0, d_ff, FF_CHUNK))
    return [(s, min(FF_CHUNK, d_ff - s)) for s in starts]


def _moe_kernel(x_ref, h_ref, gates_ref, wg_ref, wu_ref, wd_ref, gf_ref, o_ref, *, final):
    e = pl.program_id(1)

    @pl.when(e == 0)
    def _():
        o_ref[...] = x_ref[...]

    lane = lax.broadcasted_iota(jnp.int32, gates_ref.shape, 1)
    gate = jnp.sum(jnp.where(lane == e, gates_ref[...], 0.0), axis=-1, keepdims=True)
    h = h_ref[...]
    for s, w in _ff_chunks(wg_ref.shape[2]):
        g = jnp.dot(h, wg_ref[0, :, s:s + w], preferred_element_type=F32)
        u = jnp.dot(h, wu_ref[0, :, s:s + w], preferred_element_type=F32)
        act = (jax.nn.silu(g) * u * gate).astype(BF16)
        o_ref[...] += jnp.dot(act, wd_ref[0, s:s + w, :], preferred_element_type=F32)

    if final:
        @pl.when(e == pl.num_programs(1) - 1)
        def _():
            o_ref[...] = _final_norm(o_ref[...], gf_ref[...])


def _moe(x, h, gates, wg, wu, wd, g_final, final, tm):
    n, d = x.shape
    n_exp, _, d_ff = wg.shape
    row = lambda width: pl.BlockSpec((tm, width), lambda i, e: (i, 0))
    return pl.pallas_call(
        functools.partial(_moe_kernel, final=final),
        out_shape=jax.ShapeDtypeStruct((n, d), F32),
        grid=(n // tm, n_exp),
        in_specs=[
            row(d), row(d), row(LANES),
            pl.BlockSpec((1, d, d_ff), lambda i, e: (e, 0, 0)),
            pl.BlockSpec((1, d, d_ff), lambda i, e: (e, 0, 0)),
            pl.BlockSpec((1, d_ff, d), lambda i, e: (e, 0, 0)),
            pl.BlockSpec((1, d), lambda i, e: (0, 0)),
        ],
        out_specs=row(d),
        compiler_params=_params("parallel", "arbitrary"),
        name="expert_swiglu",
    )(x, h, gates, wg, wu, wd, g_final)


def _pick_tile(n, want):
    tm = min(n, want)
    assert n % tm == 0 and tm % BLK == 0, (n, tm)
    return tm


def kernel(x, w_in, w_out, g_mix, g_ffn, g_sgu, sgu_w, sgu_b, g_out, ffn_w_gate, ffn_w_up,
           ffn_w_down, router_w, moe_w_gate, moe_w_up, moe_w_down, g_final):
    batch, seq, d = x.shape
    n = batch * seq
    depth = w_in.shape[0]
    n_groups = sgu_w.shape[1]
    d_sgu = n_groups * HEAD_DIM
    d_in = w_in.shape[2]
    d_sb = (d_in - 2 * d_sgu) // 3
    n_experts = router_w.shape[2]
    assert seq % BLK == 0 and sgu_w.shape[2] == BLK and d_sb % LANES == 0 and d_sgu % LANES == 0
    assert 3 * d_sb % d_sgu == 0 and d % LANES == 0

    col_scale = jnp.where(jnp.arange(d_in) < d_sb, HEAD_DIM ** -0.5, 1.0).astype(F32)[None, :]
    row2 = lambda a: a.reshape(1, -1).astype(F32)
    g_fin = row2(g_final)

    xf = x.reshape(n, d)
    for l in range(depth):
        last = l == depth - 1
        proj = _in_proj(xf, row2(g_mix[l]), col_scale, w_in[l].astype(BF16), _pick_tile(n, 512))
        y_sb = _attention(proj, row2(g_out[l, :d_sb]), batch, seq, d_sb)
        bias = jnp.repeat(sgu_b[l].T, HEAD_DIM, axis=1)
        y_sgu = _sgu(proj, sgu_w[l], bias, row2(g_sgu[l]), row2(g_out[l, d_sb:]), d_sb, d_sgu,
                     _pick_tile(n, 512))
        i = l // 2
        if l % 2 == 0:
            xf, h2 = _out_proj(y_sb, y_sgu, xf, w_out[l].astype(BF16), row2(g_ffn[l]), None, 0,
                               _pick_tile(n, 512))
            d_ff = ffn_w_gate.shape[2]
            assert d_ff % FF_CHUNK == 0
            nc = d_ff // FF_CHUNK
            wg = ffn_w_gate[i].astype(BF16).reshape(d, nc, FF_CHUNK).transpose(1, 0, 2)
            wu = ffn_w_up[i].astype(BF16).reshape(d, nc, FF_CHUNK).transpose(1, 0, 2)
            wd = ffn_w_down[i].astype(BF16).reshape(nc, FF_CHUNK, d)
            xf = _ffn(xf, h2, wg, wu, wd, g_fin, last, _pick_tile(n, 512))
        else:
            rw = jnp.pad(router_w[i].astype(F32), ((0, 0), (0, LANES - n_experts)))
            xf, h2, gates = _out_proj(y_sb, y_sgu, xf, w_out[l].astype(BF16), row2(g_ffn[l]), rw,
                                      n_experts, _pick_tile(n, 512))
            xf = _moe(xf, h2, gates, moe_w_gate[i].astype(BF16), moe_w_up[i].astype(BF16),
                      moe_w_down[i].astype(BF16), g_fin, last, _pick_tile(n, 1024))
    return xf.reshape(batch, seq, d)
```

```python
import functools

import jax
import jax.numpy as jnp
from jax import lax
from jax.experimental import pallas as pl
from jax.experimental.pallas import tpu as pltpu

HEAD_DIM = 64
BLK = 128
LANES = 128
EPS = 1e-6
FF_CHUNK = 256
VMEM_LIMIT = 56 * 1024 * 1024

F32 = jnp.float32
BF16 = jnp.bfloat16


def _params(*sem):
    return pltpu.CompilerParams(dimension_semantics=sem, vmem_limit_bytes=VMEM_LIMIT)


def _split_bf16(a):
    hi = a.astype(BF16)
    lo = (a - hi.astype(F32)).astype(BF16)
    return hi, lo


def _same_head_mean_matrix():
    r = lax.broadcasted_iota(jnp.int32, (LANES, LANES), 0)
    c = lax.broadcasted_iota(jnp.int32, (LANES, LANES), 1)
    return jnp.where((r < HEAD_DIM) == (c < HEAD_DIM), 1.0 / HEAD_DIM, 0.0).astype(BF16)


def _head_rms_norm(y, mean_mat, gain):
    hi, lo = _split_bf16(y * y)
    ms = (jnp.dot(hi, mean_mat, preferred_element_type=F32)
          + jnp.dot(lo, mean_mat, preferred_element_type=F32))
    return y * lax.rsqrt(ms + EPS) * gain


def _in_proj_kernel(x_ref, g_ref, cs_ref, w_ref, o_ref, *, n_chunk):
    x = x_ref[...]
    ms = jnp.mean(x * x, axis=-1, keepdims=True)
    h = (x * lax.rsqrt(ms + EPS) * g_ref[...]).astype(BF16)
    for c in range(o_ref.shape[1] // n_chunk):
        sl = slice(c * n_chunk, (c + 1) * n_chunk)
        acc = jnp.dot(h, w_ref[:, sl], preferred_element_type=F32)
        o_ref[:, sl] = (acc * cs_ref[:, sl]).astype(o_ref.dtype)


def _in_proj(x, g, col_scale, w, tm):
    n, d = x.shape
    d_in = w.shape[1]
    return pl.pallas_call(
        functools.partial(_in_proj_kernel, n_chunk=512),
        out_shape=jax.ShapeDtypeStruct((n, d_in), BF16),
        grid=(n // tm,),
        in_specs=[
            pl.BlockSpec((tm, d), lambda i: (i, 0)),
            pl.BlockSpec((1, d), lambda i: (0, 0)),
            pl.BlockSpec((1, d_in), lambda i: (0, 0)),
            pl.BlockSpec((d, d_in), lambda i: (0, 0), pipeline_mode=pl.Buffered(1)),
        ],
        out_specs=pl.BlockSpec((tm, d_in), lambda i: (i, 0)),
        compiler_params=_params("parallel"),
        name="in_proj",
    )(x, g, col_scale, w)


def _attn_kernel(q_ref, k_ref, v_ref, g_ref, o_ref, q2_ref, v0_ref, v1_ref, acc_ref, z_ref, a_ref,
                 *, seq, pairs):
    nq = seq // BLK
    row = lax.broadcasted_iota(jnp.int32, (BLK, BLK), 0)
    col = lax.broadcasted_iota(jnp.int32, (BLK, BLK), 1)
    upper = jnp.where(row > col, 1.0, 0.0).astype(BF16)
    upper2 = jnp.concatenate([upper, upper], axis=0)
    causal = jnp.concatenate([col < row, col < row], axis=0)
    mean_mat = _same_head_mean_matrix()
    lane = lax.broadcasted_iota(jnp.int32, (1, pairs * LANES), 1) % LANES
    m0 = jnp.where(lane < HEAD_DIM, 1.0, 0.0).astype(BF16)
    m1 = jnp.where(lane < HEAD_DIM, 0.0, 1.0).astype(BF16)

    v0_ref[...] = v_ref[...] * m0
    v1_ref[...] = v_ref[...] * m1
    for p in range(pairs):
        ls = slice(p * LANES, (p + 1) * LANES)
        q2_ref[p, :, :BLK, :] = (q_ref[:, ls] * m0[:, ls]).reshape(nq, BLK, LANES)
        q2_ref[p, :, BLK:, :] = (q_ref[:, ls] * m1[:, ls]).reshape(nq, BLK, LANES)

    lanes = [slice(p * LANES, (p + 1) * LANES) for p in range(pairs)]

    def scores(i, j, slot):
        ks = pl.multiple_of(j * BLK, BLK)
        for p in range(pairs):
            z_ref[slot, p] = lax.dot_general(q2_ref[p, i], k_ref[pl.ds(ks, BLK), lanes[p]],
                                             (((1,), (1,)), ((), ())),
                                             preferred_element_type=F32)

    def weighted_values(j):
        ks = pl.multiple_of(j * BLK, BLK)
        for p in range(pairs):
            acc_ref[p] += (
                jnp.dot(a_ref[p, :BLK], v0_ref[pl.ds(ks, BLK), lanes[p]], preferred_element_type=F32)
                + jnp.dot(a_ref[p, BLK:], v1_ref[pl.ds(ks, BLK), lanes[p]], preferred_element_type=F32))

    def weights(zs, carries, mask):
        log_betas, log_oms, splits = [], [], []
        for z in zs:
            sp = jnp.maximum(z, 0.0) + jnp.log(1.0 + jnp.exp(-jnp.abs(z)))
            log_om = -sp
            if mask is not None:
                log_om = jnp.where(mask, log_om, 0.0)
            log_betas.append(z - sp)
            log_oms.append(log_om)
            splits.append(jnp.concatenate(_split_bf16(log_om), axis=1))
        betweens = [jnp.dot(s, upper2, preferred_element_type=F32) for s in splits]
        out = []
        for p in range(pairs):
            between = betweens[p] if carries is None else betweens[p] + carries[p]
            a = jnp.exp(log_betas[p] + between)
            if mask is not None:
                a = jnp.where(mask, a, 0.0)
            a_ref[p] = a.astype(BF16)
            rowsum = jnp.sum(log_oms[p], axis=-1, keepdims=True)
            out.append(rowsum if carries is None else carries[p] + rowsum)
        return tuple(out)

    def q_block(i, carry):
        acc_ref[...] = jnp.zeros_like(acc_ref)
        scores(i, i, 0)
        scores(i, jnp.maximum(i - 1, 0), 1)
        carries = weights([z_ref[0, p] for p in range(pairs)], None, causal)

        def key_block(jj, cs):
            j = i - 1 - jj
            slot = (jj + 1) % 2
            zs = [z_ref[slot, p] for p in range(pairs)]
            weighted_values(j + 1)
            scores(i, jnp.maximum(j - 1, 0), 1 - slot)
            return weights(zs, cs, None)

        lax.fori_loop(0, i, key_block, carries)
        weighted_values(0)
        qs = pl.multiple_of(i * BLK, BLK)
        for p in range(pairs):
            ls = slice(p * LANES, (p + 1) * LANES)
            y = _head_rms_norm(acc_ref[p], mean_mat, g_ref[:, ls])
            o_ref[pl.ds(qs, BLK), ls] = y.astype(o_ref.dtype)
        return carry

    lax.fori_loop(0, nq, q_block, 0)


def _attention(proj, g_sb, batch, seq, d_sb):
    n = proj.shape[0]
    pairs = d_sb // LANES
    blk = lambda off: pl.BlockSpec((seq, d_sb), lambda b: (b, off))
    return pl.pallas_call(
        functools.partial(_attn_kernel, seq=seq, pairs=pairs),
        out_shape=jax.ShapeDtypeStruct((n, d_sb), BF16),
        grid=(batch,),
        in_specs=[blk(0), blk(1), blk(2), pl.BlockSpec((1, d_sb), lambda b: (0, 0))],
        out_specs=pl.BlockSpec((seq, d_sb), lambda b: (b, 0)),
        scratch_shapes=[pltpu.VMEM((pairs, seq // BLK, 2 * BLK, LANES), BF16),
                        pltpu.VMEM((seq, d_sb), BF16),
                        pltpu.VMEM((seq, d_sb), BF16),
                        pltpu.VMEM((pairs, BLK, LANES), F32),
                        pltpu.VMEM((2, pairs, 2 * BLK, LANES), F32),
                        pltpu.VMEM((pairs, 2 * BLK, LANES), BF16)],
        compiler_params=_params("parallel"),
        name="stickbreak_attn",
    )(proj, proj, proj, g_sb)


def _sgu_kernel(u_ref, gt_ref, w_ref, b_ref, gs_ref, go_ref, o_ref):
    tm, d_sgu = o_ref.shape
    row = lax.broadcasted_iota(jnp.int32, (BLK, BLK), 0)
    col = lax.broadcasted_iota(jnp.int32, (BLK, BLK), 1)
    tril = col <= row
    head0 = col < HEAD_DIM
    mean_mat = _same_head_mean_matrix()
    for p in range(d_sgu // LANES):
        ls = slice(p * LANES, (p + 1) * LANES)
        w0 = jnp.where(tril, w_ref[2 * p], 0.0).astype(BF16)
        w1 = jnp.where(tril, w_ref[2 * p + 1], 0.0).astype(BF16)
        for c in range(tm // BLK):
            rs = slice(c * BLK, (c + 1) * BLK)
            gate = jax.nn.gelu(gt_ref[rs, ls].astype(F32))
            gate = _head_rms_norm(gate, mean_mat, gs_ref[:, ls]).astype(BF16)
            mixed = jnp.where(head0,
                              jnp.dot(w0, gate, preferred_element_type=F32),
                              jnp.dot(w1, gate, preferred_element_type=F32)) + b_ref[:, ls]
            y = jax.nn.gelu(u_ref[rs, ls].astype(F32)) * mixed
            o_ref[rs, ls] = _head_rms_norm(y, mean_mat, go_ref[:, ls]).astype(o_ref.dtype)


def _sgu(proj, w_s, bias, g_gate, g_o, d_sb, d_sgu, tm):
    n = proj.shape[0]
    u_blk = 3 * d_sb // d_sgu
    return pl.pallas_call(
        _sgu_kernel,
        out_shape=jax.ShapeDtypeStruct((n, d_sgu), BF16),
        grid=(n // tm,),
        in_specs=[
            pl.BlockSpec((tm, d_sgu), lambda i: (i, u_blk)),
            pl.BlockSpec((tm, d_sgu), lambda i: (i, u_blk + 1)),
            pl.BlockSpec(w_s.shape, lambda i: (0, 0, 0)),
            pl.BlockSpec(bias.shape, lambda i: (0, 0)),
            pl.BlockSpec((1, d_sgu), lambda i: (0, 0)),
            pl.BlockSpec((1, d_sgu), lambda i: (0, 0)),
        ],
        out_specs=pl.BlockSpec((tm, d_sgu), lambda i: (i, 0)),
        compiler_params=_params("parallel"),
        name="spatial_gating",
    )(proj, proj, w_s, bias, g_gate, g_o)


def _top2_gates(logits, n_experts):
    lane = lax.broadcasted_iota(jnp.int32, logits.shape, 1).astype(F32)
    neg = jnp.float32(-jnp.inf)
    lg = jnp.where(lane < n_experts, logits, neg)
    m1 = jnp.max(lg, axis=-1, keepdims=True)
    i1 = jnp.min(jnp.where(lg == m1, lane, float(LANES)), axis=-1, keepdims=True)
    lg2 = jnp.where(lane == i1, neg, lg)
    m2 = jnp.max(lg2, axis=-1, keepdims=True)
    i2 = jnp.min(jnp.where(lg2 == m2, lane, float(LANES)), axis=-1, keepdims=True)
    e2 = jnp.exp(m2 - m1)
    w1 = 1.0 / (1.0 + e2)
    w2 = e2 / (1.0 + e2)
    return jnp.where(lane == i1, w1, 0.0) + jnp.where(lane == i2, w2, 0.0)


def _out_proj_kernel(*refs, n_experts):
    if n_experts:
        ysb_ref, ysgu_ref, x_ref, w_ref, g_ref, rw_ref, xo_ref, h_ref, gates_ref = refs
    else:
        ysb_ref, ysgu_ref, x_ref, w_ref, g_ref, xo_ref, h_ref = refs
    y = jnp.concatenate([ysb_ref[...], ysgu_ref[...]], axis=1)
    xn = x_ref[...] + jnp.dot(y, w_ref[...], preferred_element_type=F32)
    xo_ref[...] = xn
    ms = jnp.mean(xn * xn, axis=-1, keepdims=True)
    h = xn * lax.rsqrt(ms + EPS) * g_ref[...]
    h_ref[...] = h.astype(h_ref.dtype)
    if n_experts:
        h_hi, h_lo = _split_bf16(h)
        w_hi, w_lo = _split_bf16(rw_ref[...])
        logits = (jnp.dot(h_hi, w_hi, preferred_element_type=F32)
                  + jnp.dot(h_lo, w_hi, preferred_element_type=F32)
                  + jnp.dot(h_hi, w_lo, preferred_element_type=F32))
        gates_ref[...] = _top2_gates(logits, n_experts)


def _out_proj(ysb, ysgu, x, w, g, router_w, n_experts, tm):
    n, d = x.shape
    d_half = ysb.shape[1]
    row_blk = lambda width: pl.BlockSpec((tm, width), lambda i: (i, 0))
    const = lambda shape: pl.BlockSpec(shape, lambda i: (0, 0))
    in_specs = [row_blk(d_half), row_blk(ysgu.shape[1]), row_blk(d), const(w.shape), const((1, d))]
    out_shape = [jax.ShapeDtypeStruct((n, d), F32), jax.ShapeDtypeStruct((n, d), BF16)]
    out_specs = [row_blk(d), row_blk(d)]
    args = [ysb, ysgu, x, w, g]
    if n_experts:
        in_specs.append(const(router_w.shape))
        out_shape.append(jax.ShapeDtypeStruct((n, LANES), F32))
        out_specs.append(row_blk(LANES))
        args.append(router_w)
    return pl.pallas_call(
        functools.partial(_out_proj_kernel, n_experts=n_experts),
        out_shape=out_shape,
        grid=(n // tm,),
        in_specs=in_specs,
        out_specs=out_specs,
        compiler_params=_params("parallel"),
        name="out_proj_router" if n_experts else "out_proj",
    )(*args)


def _final_norm(x, gain):
    ms = jnp.mean(x * x, axis=-1, keepdims=True)
    return x * lax.rsqrt(ms + EPS) * gain


def _ffn_kernel(x_ref, h_ref, wg_ref, wu_ref, wd_ref, gf_ref, o_ref, *, final):
    h = h_ref[...]
    o_ref[...] = x_ref[...]

    def chunk(c, carry):
        g = jnp.dot(h, wg_ref[c], preferred_element_type=F32)
        u = jnp.dot(h, wu_ref[c], preferred_element_type=F32)
        act = (jax.nn.silu(g) * u).astype(BF16)
        o_ref[...] += jnp.dot(act, wd_ref[c], preferred_element_type=F32)
        return carry

    lax.fori_loop(0, wg_ref.shape[0], chunk, 0)
    if final:
        o_ref[...] = _final_norm(o_ref[...], gf_ref[...])


def _ffn(x, h, wg, wu, wd, g_final, final, tm):
    n, d = x.shape
    const3 = lambda a: pl.BlockSpec(a.shape, lambda i: (0, 0, 0), pipeline_mode=pl.Buffered(1))
    return pl.pallas_call(
        functools.partial(_ffn_kernel, final=final),
        out_shape=jax.ShapeDtypeStruct((n, d), F32),
        grid=(n // tm,),
        in_specs=[
            pl.BlockSpec((tm, d), lambda i: (i, 0)),
            pl.BlockSpec((tm, d), lambda i: (i, 0)),
            const3(wg), const3(wu), const3(wd),
            pl.BlockSpec((1, d), lambda i: (0, 0)),
        ],
        out_specs=pl.BlockSpec((tm, d), lambda i: (i, 0)),
        compiler_params=_params("parallel"),
        name="dense_swiglu",
    )(x, h, wg, wu, wd, g_final)


def _ff_chunks(d_ff):
    starts = list(range(0, d_ff, FF_CHUNK))
    return [(s, min(FF_CHUNK, d_ff - s)) for s in starts]


def _moe_kernel(x_ref, h_ref, gates_ref, wg_ref, wu_ref, wd_ref, gf_ref, o_ref, *, final):
    e = pl.program_id(1)

    @pl.when(e == 0)
    def _():
        o_ref[...] = x_ref[...]

    lane = lax.broadcasted_iota(jnp.int32, gates_ref.shape, 1)
    gate = jnp.sum(jnp.where(lane == e, gates_ref[...], 0.0), axis=-1, keepdims=True)
    h = h_ref[...]
    for s, w in _ff_chunks(wg_ref.shape[2]):
        g = jnp.dot(h, wg_ref[0, :, s:s + w], preferred_element_type=F32)
        u = jnp.dot(h, wu_ref[0, :, s:s + w], preferred_element_type=F32)
        act = (jax.nn.silu(g) * u * gate).astype(BF16)
        o_ref[...] += jnp.dot(act, wd_ref[0, s:s + w, :], preferred_element_type=F32)

    if final:
        @pl.when(e == pl.num_programs(1) - 1)
        def _():
            o_ref[...] = _final_norm(o_ref[...], gf_ref[...])


def _moe(x, h, gates, wg, wu, wd, g_final, final, tm):
    n, d = x.shape
    n_exp, _, d_ff = wg.shape
    row = lambda width: pl.BlockSpec((tm, width), lambda i, e: (i, 0))
    return pl.pallas_call(
        functools.partial(_moe_kernel, final=final),
        out_shape=jax.ShapeDtypeStruct((n, d), F32),
        grid=(n // tm, n_exp),
        in_specs=[
            row(d), row(d), row(LANES),
            pl.BlockSpec((1, d, d_ff), lambda i, e: (e, 0, 0)),
            pl.BlockSpec((1, d, d_ff), lambda i, e: (e, 0, 0)),
            pl.BlockSpec((1, d_ff, d), lambda i, e: (e, 0, 0)),
            pl.BlockSpec((1, d), lambda i, e: (0, 0)),
        ],
        out_specs=row(d),
        compiler_params=_params("parallel", "arbitrary"),
        name="expert_swiglu",
    )(x, h, gates, wg, wu, wd, g_final)


def _pick_tile(n, want):
    tm = min(n, want)
    assert n % tm == 0 and tm % BLK == 0, (n, tm)
    return tm


def kernel(x, w_in, w_out, g_mix, g_ffn, g_sgu, sgu_w, sgu_b, g_out, ffn_w_gate, ffn_w_up,
           ffn_w_down, router_w, moe_w_gate, moe_w_up, moe_w_down, g_final):
    batch, seq, d = x.shape
    n = batch * seq
    depth = w_in.shape[0]
    n_groups = sgu_w.shape[1]
    d_sgu = n_groups * HEAD_DIM
    d_in = w_in.shape[2]
    d_sb = (d_in - 2 * d_sgu) // 3
    n_experts = router_w.shape[2]
    assert seq % BLK == 0 and sgu_w.shape[2] == BLK and d_sb % LANES == 0 and d_sgu % LANES == 0
    assert 3 * d_sb % d_sgu == 0 and d % LANES == 0

    col_scale = jnp.where(jnp.arange(d_in) < d_sb, HEAD_DIM ** -0.5, 1.0).astype(F32)[None, :]
    row2 = lambda a: a.reshape(1, -1).astype(F32)
    g_fin = row2(g_final)

    xf = x.reshape(n, d)
    for l in range(depth):
        last = l == depth - 1
        proj = _in_proj(xf, row2(g_mix[l]), col_scale, w_in[l].astype(BF16), _pick_tile(n, 512))
        y_sb = _attention(proj, row2(g_out[l, :d_sb]), batch, seq, d_sb)
        bias = jnp.repeat(sgu_b[l].T, HEAD_DIM, axis=1)
        y_sgu = _sgu(proj, sgu_w[l], bias, row2(g_sgu[l]), row2(g_out[l, d_sb:]), d_sb, d_sgu,
                     _pick_tile(n, 512))
        i = l // 2
        if l % 2 == 0:
            xf, h2 = _out_proj(y_sb, y_sgu, xf, w_out[l].astype(BF16), row2(g_ffn[l]), None, 0,
                               _pick_tile(n, 512))
            d_ff = ffn_w_gate.shape[2]
            assert d_ff % FF_CHUNK == 0
            nc = d_ff // FF_CHUNK
            wg = ffn_w_gate[i].astype(BF16).reshape(d, nc, FF_CHUNK).transpose(1, 0, 2)
            wu = ffn_w_up[i].astype(BF16).reshape(d, nc, FF_CHUNK).transpose(1, 0, 2)
            wd = ffn_w_down[i].astype(BF16).reshape(nc, FF_CHUNK, d)
            xf = _ffn(xf, h2, wg, wu, wd, g_fin, last, _pick_tile(n, 512))
        else:
            rw = jnp.pad(router_w[i].astype(F32), ((0, 0), (0, LANES - n_experts)))
            xf, h2, gates = _out_proj(y_sb, y_sgu, xf, w_out[l].astype(BF16), row2(g_ffn[l]), rw,
                                      n_experts, _pick_tile(n, 512))
            xf = _moe(xf, h2, gates, moe_w_gate[i].astype(BF16), moe_w_up[i].astype(BF16),
                      moe_w_down[i].astype(BF16), g_fin, last, _pick_tile(n, 1024))
    return xf.reshape(batch, seq, d)
```

```python
import functools

import jax
import jax.numpy as jnp
from jax import lax
from jax.experimental import pallas as pl
from jax.experimental.pallas import tpu as pltpu

HEAD_DIM = 64
BLK = 128
LANES = 128
EPS = 1e-6
LOG2E = 1.4426950408889634
UNDERFLOW_LOG = -104.0
FF_CHUNK = 256
VMEM_LIMIT = 56 * 1024 * 1024

F32 = jnp.float32
BF16 = jnp.bfloat16


def _params(*sem):
    return pltpu.CompilerParams(dimension_semantics=sem, vmem_limit_bytes=VMEM_LIMIT)


def _split_bf16(a):
    hi = a.astype(BF16)
    lo = (a - hi.astype(F32)).astype(BF16)
    return hi, lo


def _same_head_mean_matrix():
    r = lax.broadcasted_iota(jnp.int32, (LANES, LANES), 0)
    c = lax.broadcasted_iota(jnp.int32, (LANES, LANES), 1)
    return jnp.where((r < HEAD_DIM) == (c < HEAD_DIM), 1.0 / HEAD_DIM, 0.0).astype(BF16)


def _head_rms_norm(y, mean_mat, gain):
    hi, lo = _split_bf16(y * y)
    ms = (jnp.dot(hi, mean_mat, preferred_element_type=F32)
          + jnp.dot(lo, mean_mat, preferred_element_type=F32))
    return y * lax.rsqrt(ms + EPS) * gain


def _in_proj_kernel(x_ref, g_ref, cs_ref, w_ref, o_ref, *, n_chunk):
    x = x_ref[...]
    ms = jnp.mean(x * x, axis=-1, keepdims=True)
    h = (x * lax.rsqrt(ms + EPS) * g_ref[...]).astype(BF16)
    for c in range(o_ref.shape[1] // n_chunk):
        sl = slice(c * n_chunk, (c + 1) * n_chunk)
        acc = jnp.dot(h, w_ref[:, sl], preferred_element_type=F32)
        o_ref[:, sl] = (acc * cs_ref[:, sl]).astype(o_ref.dtype)


def _in_proj(x, g, col_scale, w, tm):
    n, d = x.shape
    d_in = w.shape[1]
    return pl.pallas_call(
        functools.partial(_in_proj_kernel, n_chunk=512),
        out_shape=jax.ShapeDtypeStruct((n, d_in), BF16),
        grid=(n // tm,),
        in_specs=[
            pl.BlockSpec((tm, d), lambda i: (i, 0)),
            pl.BlockSpec((1, d), lambda i: (0, 0)),
            pl.BlockSpec((1, d_in), lambda i: (0, 0)),
            pl.BlockSpec((d, d_in), lambda i: (0, 0), pipeline_mode=pl.Buffered(1)),
        ],
        out_specs=pl.BlockSpec((tm, d_in), lambda i: (i, 0)),
        compiler_params=_params("parallel"),
        name="in_proj",
    )(x, g, col_scale, w)


def _attn_kernel(q_ref, k_ref, v_ref, g_ref, o_ref, q2_ref, v0_ref, v1_ref, acc_ref, z_ref, a_ref,
                 *, seq, pairs):
    nq = seq // BLK
    row = lax.broadcasted_iota(jnp.int32, (BLK, BLK), 0)
    col = lax.broadcasted_iota(jnp.int32, (BLK, BLK), 1)
    upper = jnp.where(row > col, 1.0, 0.0).astype(BF16)
    upper2 = jnp.concatenate([upper, upper], axis=0)
    causal = jnp.concatenate([col < row, col < row], axis=0)
    mean_mat = _same_head_mean_matrix()
    lane = lax.broadcasted_iota(jnp.int32, (1, pairs * LANES), 1) % LANES
    m0 = jnp.where(lane < HEAD_DIM, 1.0, 0.0).astype(BF16)
    m1 = jnp.where(lane < HEAD_DIM, 0.0, 1.0).astype(BF16)

    v0_ref[...] = v_ref[...] * m0
    v1_ref[...] = v_ref[...] * m1
    for p in range(pairs):
        ls = slice(p * LANES, (p + 1) * LANES)
        q2_ref[p, :, :BLK, :] = (q_ref[:, ls] * m0[:, ls]).reshape(nq, BLK, LANES)
        q2_ref[p, :, BLK:, :] = (q_ref[:, ls] * m1[:, ls]).reshape(nq, BLK, LANES)

    lanes = [slice(p * LANES, (p + 1) * LANES) for p in range(pairs)]

    def scores(i, j, slot):
        ks = pl.multiple_of(j * BLK, BLK)
        for p in range(pairs):
            z_ref[slot, p] = lax.dot_general(q2_ref[p, i], k_ref[pl.ds(ks, BLK), lanes[p]],
                                             (((1,), (1,)), ((), ())),
                                             preferred_element_type=F32)

    def weighted_values(j):
        ks = pl.multiple_of(j * BLK, BLK)
        for p in range(pairs):
            acc_ref[p] += (
                jnp.dot(a_ref[p, :BLK], v0_ref[pl.ds(ks, BLK), lanes[p]], preferred_element_type=F32)
                + jnp.dot(a_ref[p, BLK:], v1_ref[pl.ds(ks, BLK), lanes[p]], preferred_element_type=F32))

    def weights(zs, carries, mask):
        log_betas, sps, splits = [], [], []
        for z in zs:
            e = jnp.exp2(jnp.abs(z) * (-LOG2E))
            sp = jnp.maximum(z, 0.0) + jnp.log(1.0 + e)
            log_betas.append(z - sp)
            if mask is not None:
                sp = jnp.where(mask, sp, 0.0)
            sps.append(sp)
            splits.append(jnp.concatenate(_split_bf16(sp), axis=1))
        new_carries = []
        for p in range(pairs):
            rowsum = jnp.sum(sps[p], axis=-1, keepdims=True)
            new_carries.append(rowsum if carries is None else carries[p] + rowsum)
        if carries is None:
            alive = jnp.int32(1)
        else:
            low = functools.reduce(jnp.minimum, new_carries)
            alive = (jnp.min(low) < -UNDERFLOW_LOG).astype(jnp.int32)
        betweens = [jnp.dot(s, upper2, preferred_element_type=F32) for s in splits]
        for p in range(pairs):
            between = betweens[p] if carries is None else betweens[p] + carries[p]
            a = jnp.exp(log_betas[p] - between)
            if mask is not None:
                a = jnp.where(mask, a, 0.0)
            a_ref[p] = a.astype(BF16)
        return alive, tuple(new_carries)

    def q_block(i, carry):
        acc_ref[...] = jnp.zeros_like(acc_ref)
        scores(i, i, 0)
        scores(i, jnp.maximum(i - 1, 0), 1)
        alive, carries = weights([z_ref[0, p] for p in range(pairs)], None, causal)

        def more_keys(state):
            return jnp.logical_and(state[0] < i, state[1] > 0)

        def key_block(state):
            jj, cs = state[0], state[2]
            j = i - 1 - jj
            slot = (jj + 1) % 2
            zs = [z_ref[slot, p] for p in range(pairs)]
            weighted_values(j + 1)
            scores(i, jnp.maximum(j - 1, 0), 1 - slot)
            alive, cs = weights(zs, cs, None)
            return jj + 1, alive, cs

        done = lax.while_loop(more_keys, key_block, (jnp.int32(0), alive, carries))[0]
        weighted_values(i - done)
        qs = pl.multiple_of(i * BLK, BLK)
        for p in range(pairs):
            ls = slice(p * LANES, (p + 1) * LANES)
            y = _head_rms_norm(acc_ref[p], mean_mat, g_ref[:, ls])
            o_ref[pl.ds(qs, BLK), ls] = y.astype(o_ref.dtype)
        return carry

    lax.fori_loop(0, nq, q_block, 0)


def _attention(proj, g_sb, batch, seq, d_sb):
    n = proj.shape[0]
    pairs = d_sb // LANES
    blk = lambda off: pl.BlockSpec((seq, d_sb), lambda b: (b, off))
    return pl.pallas_call(
        functools.partial(_attn_kernel, seq=seq, pairs=pairs),
        out_shape=jax.ShapeDtypeStruct((n, d_sb), BF16),
        grid=(batch,),
        in_specs=[blk(0), blk(1), blk(2), pl.BlockSpec((1, d_sb), lambda b: (0, 0))],
        out_specs=pl.BlockSpec((seq, d_sb), lambda b: (b, 0)),
        scratch_shapes=[pltpu.VMEM((pairs, seq // BLK, 2 * BLK, LANES), BF16),
                        pltpu.VMEM((seq, d_sb), BF16),
                        pltpu.VMEM((seq, d_sb), BF16),
                        pltpu.VMEM((pairs, BLK, LANES), F32),
                        pltpu.VMEM((2, pairs, 2 * BLK, LANES), F32),
                        pltpu.VMEM((pairs, 2 * BLK, LANES), BF16)],
        compiler_params=_params("parallel"),
        name="stickbreak_attn",
    )(proj, proj, proj, g_sb)


def _sgu_kernel(u_ref, gt_ref, w_ref, b_ref, gs_ref, go_ref, o_ref):
    tm, d_sgu = o_ref.shape
    row = lax.broadcasted_iota(jnp.int32, (BLK, BLK), 0)
    col = lax.broadcasted_iota(jnp.int32, (BLK, BLK), 1)
    tril = col <= row
    head0 = col < HEAD_DIM
    mean_mat = _same_head_mean_matrix()
    for p in range(d_sgu // LANES):
        ls = slice(p * LANES, (p + 1) * LANES)
        w0 = jnp.where(tril, w_ref[2 * p], 0.0).astype(BF16)
        w1 = jnp.where(tril, w_ref[2 * p + 1], 0.0).astype(BF16)
        for c in range(tm // BLK):
            rs = slice(c * BLK, (c + 1) * BLK)
            gate = jax.nn.gelu(gt_ref[rs, ls].astype(F32))
            gate = _head_rms_norm(gate, mean_mat, gs_ref[:, ls]).astype(BF16)
            mixed = jnp.where(head0,
                              jnp.dot(w0, gate, preferred_element_type=F32),
                              jnp.dot(w1, gate, preferred_element_type=F32)) + b_ref[:, ls]
            y = jax.nn.gelu(u_ref[rs, ls].astype(F32)) * mixed
            o_ref[rs, ls] = _head_rms_norm(y, mean_mat, go_ref[:, ls]).astype(o_ref.dtype)


def _sgu(proj, w_s, bias, g_gate, g_o, d_sb, d_sgu, tm):
    n = proj.shape[0]
    u_blk = 3 * d_sb // d_sgu
    return pl.pallas_call(
        _sgu_kernel,
        out_shape=jax.ShapeDtypeStruct((n, d_sgu), BF16),
        grid=(n // tm,),
        in_specs=[
            pl.BlockSpec((tm, d_sgu), lambda i: (i, u_blk)),
            pl.BlockSpec((tm, d_sgu), lambda i: (i, u_blk + 1)),
            pl.BlockSpec(w_s.shape, lambda i: (0, 0, 0)),
            pl.BlockSpec(bias.shape, lambda i: (0, 0)),
            pl.BlockSpec((1, d_sgu), lambda i: (0, 0)),
            pl.BlockSpec((1, d_sgu), lambda i: (0, 0)),
        ],
        out_specs=pl.BlockSpec((tm, d_sgu), lambda i: (i, 0)),
        compiler_params=_params("parallel"),
        name="spatial_gating",
    )(proj, proj, w_s, bias, g_gate, g_o)


def _top2_gates(logits, n_experts):
    lane = lax.broadcasted_iota(jnp.int32, logits.shape, 1).astype(F32)
    neg = jnp.float32(-jnp.inf)
    lg = jnp.where(lane < n_experts, logits, neg)
    m1 = jnp.max(lg, axis=-1, keepdims=True)
    i1 = jnp.min(jnp.where(lg == m1, lane, float(LANES)), axis=-1, keepdims=True)
    lg2 = jnp.where(lane == i1, neg, lg)
    m2 = jnp.max(lg2, axis=-1, keepdims=True)
    i2 = jnp.min(jnp.where(lg2 == m2, lane, float(LANES)), axis=-1, keepdims=True)
    e2 = jnp.exp(m2 - m1)
    w1 = 1.0 / (1.0 + e2)
    w2 = e2 / (1.0 + e2)
    return jnp.where(lane == i1, w1, 0.0) + jnp.where(lane == i2, w2, 0.0)


def _out_proj_kernel(*refs, n_experts):
    if n_experts:
        ysb_ref, ysgu_ref, x_ref, w_ref, g_ref, rw_ref, xo_ref, h_ref, gates_ref = refs
    else:
        ysb_ref, ysgu_ref, x_ref, w_ref, g_ref, xo_ref, h_ref = refs
    y = jnp.concatenate([ysb_ref[...], ysgu_ref[...]], axis=1)
    xn = x_ref[...] + jnp.dot(y, w_ref[...], preferred_element_type=F32)
    xo_ref[...] = xn
    ms = jnp.mean(xn * xn, axis=-1, keepdims=True)
    h = xn * lax.rsqrt(ms + EPS) * g_ref[...]
    h_ref[...] = h.astype(h_ref.dtype)
    if n_experts:
        h_hi, h_lo = _split_bf16(h)
        w_hi, w_lo = _split_bf16(rw_ref[...])
        logits = (jnp.dot(h_hi, w_hi, preferred_element_type=F32)
                  + jnp.dot(h_lo, w_hi, preferred_element_type=F32)
                  + jnp.dot(h_hi, w_lo, preferred_element_type=F32))
        gates_ref[...] = _top2_gates(logits, n_experts)


def _out_proj(ysb, ysgu, x, w, g, router_w, n_experts, tm):
    n, d = x.shape
    d_half = ysb.shape[1]
    row_blk = lambda width: pl.BlockSpec((tm, width), lambda i: (i, 0))
    const = lambda shape: pl.BlockSpec(shape, lambda i: (0, 0))
    in_specs = [row_blk(d_half), row_blk(ysgu.shape[1]), row_blk(d), const(w.shape), const((1, d))]
    out_shape = [jax.ShapeDtypeStruct((n, d), F32), jax.ShapeDtypeStruct((n, d), BF16)]
    out_specs = [row_blk(d), row_blk(d)]
    args = [ysb, ysgu, x, w, g]
    if n_experts:
        in_specs.append(const(router_w.shape))
        out_shape.append(jax.ShapeDtypeStruct((n, LANES), F32))
        out_specs.append(row_blk(LANES))
        args.append(router_w)
    return pl.pallas_call(
        functools.partial(_out_proj_kernel, n_experts=n_experts),
        out_shape=out_shape,
        grid=(n // tm,),
        in_specs=in_specs,
        out_specs=out_specs,
        compiler_params=_params("parallel"),
        name="out_proj_router" if n_experts else "out_proj",
    )(*args)


def _final_norm(x, gain):
    ms = jnp.mean(x * x, axis=-1, keepdims=True)
    return x * lax.rsqrt(ms + EPS) * gain


def _ffn_kernel(x_ref, h_ref, wg_ref, wu_ref, wd_ref, gf_ref, o_ref, *, final):
    h = h_ref[...]
    o_ref[...] = x_ref[...]

    def chunk(c, carry):
        g = jnp.dot(h, wg_ref[c], preferred_element_type=F32)
        u = jnp.dot(h, wu_ref[c], preferred_element_type=F32)
        act = (jax.nn.silu(g) * u).astype(BF16)
        o_ref[...] += jnp.dot(act, wd_ref[c], preferred_element_type=F32)
        return carry

    lax.fori_loop(0, wg_ref.shape[0], chunk, 0)
    if final:
        o_ref[...] = _final_norm(o_ref[...], gf_ref[...])


def _ffn(x, h, wg, wu, wd, g_final, final, tm):
    n, d = x.shape
    const3 = lambda a: pl.BlockSpec(a.shape, lambda i: (0, 0, 0), pipeline_mode=pl.Buffered(1))
    return pl.pallas_call(
        functools.partial(_ffn_kernel, final=final),
        out_shape=jax.ShapeDtypeStruct((n, d), F32),
        grid=(n // tm,),
        in_specs=[
            pl.BlockSpec((tm, d), lambda i: (i, 0)),
            pl.BlockSpec((tm, d), lambda i: (i, 0)),
            const3(wg), const3(wu), const3(wd),
            pl.BlockSpec((1, d), lambda i: (0, 0)),
        ],
        out_specs=pl.BlockSpec((tm, d), lambda i: (i, 0)),
        compiler_params=_params("parallel"),
        name="dense_swiglu",
    )(x, h, wg, wu, wd, g_final)


def _ff_chunks(d_ff):
    starts = list(range(0, d_ff, FF_CHUNK))
    return [(s, min(FF_CHUNK, d_ff - s)) for s in starts]


def _moe_kernel(x_ref, h_ref, gates_ref, wg_ref, wu_ref, wd_ref, gf_ref, o_ref, *, final):
    e = pl.program_id(1)

    @pl.when(e == 0)
    def _():
        o_ref[...] = x_ref[...]

    lane = lax.broadcasted_iota(jnp.int32, gates_ref.shape, 1)
    gate = jnp.sum(jnp.where(lane == e, gates_ref[...], 0.0), axis=-1, keepdims=True)
    h = h_ref[...]
    for s, w in _ff_chunks(wg_ref.shape[2]):
        g = jnp.dot(h, wg_ref[0, :, s:s + w], preferred_element_type=F32)
        u = jnp.dot(h, wu_ref[0, :, s:s + w], preferred_element_type=F32)
        act = (jax.nn.silu(g) * u * gate).astype(BF16)
        o_ref[...] += jnp.dot(act, wd_ref[0, s:s + w, :], preferred_element_type=F32)

    if final:
        @pl.when(e == pl.num_programs(1) - 1)
        def _():
            o_ref[...] = _final_norm(o_ref[...], gf_ref[...])


def _moe(x, h, gates, wg, wu, wd, g_final, final, tm):
    n, d = x.shape
    n_exp, _, d_ff = wg.shape
    row = lambda width: pl.BlockSpec((tm, width), lambda i, e: (i, 0))
    return pl.pallas_call(
        functools.partial(_moe_kernel, final=final),
        out_shape=jax.ShapeDtypeStruct((n, d), F32),
        grid=(n // tm, n_exp),
        in_specs=[
            row(d), row(d), row(LANES),
            pl.BlockSpec((1, d, d_ff), lambda i, e: (e, 0, 0)),
            pl.BlockSpec((1, d, d_ff), lambda i, e: (e, 0, 0)),
            pl.BlockSpec((1, d_ff, d), lambda i, e: (e, 0, 0)),
            pl.BlockSpec((1, d), lambda i, e: (0, 0)),
        ],
        out_specs=row(d),
        compiler_params=_params("parallel", "arbitrary"),
        name="expert_swiglu",
    )(x, h, gates, wg, wu, wd, g_final)


def _pick_tile(n, want):
    tm = min(n, want)
    assert n % tm == 0 and tm % BLK == 0, (n, tm)
    return tm


def kernel(x, w_in, w_out, g_mix, g_ffn, g_sgu, sgu_w, sgu_b, g_out, ffn_w_gate, ffn_w_up,
           ffn_w_down, router_w, moe_w_gate, moe_w_up, moe_w_down, g_final):
    batch, seq, d = x.shape
    n = batch * seq
    depth = w_in.shape[0]
    n_groups = sgu_w.shape[1]
    d_sgu = n_groups * HEAD_DIM
    d_in = w_in.shape[2]
    d_sb = (d_in - 2 * d_sgu) // 3
    n_experts = router_w.shape[2]
    assert seq % BLK == 0 and sgu_w.shape[2] == BLK and d_sb % LANES == 0 and d_sgu % LANES == 0
    assert 3 * d_sb % d_sgu == 0 and d % LANES == 0

    col_scale = jnp.where(jnp.arange(d_in) < d_sb, HEAD_DIM ** -0.5, 1.0).astype(F32)[None, :]
    row2 = lambda a: a.reshape(1, -1).astype(F32)
    g_fin = row2(g_final)

    xf = x.reshape(n, d)
    for l in range(depth):
        last = l == depth - 1
        proj = _in_proj(xf, row2(g_mix[l]), col_scale, w_in[l].astype(BF16), _pick_tile(n, 512))
        y_sb = _attention(proj, row2(g_out[l, :d_sb]), batch, seq, d_sb)
        bias = jnp.repeat(sgu_b[l].T, HEAD_DIM, axis=1)
        y_sgu = _sgu(proj, sgu_w[l], bias, row2(g_sgu[l]), row2(g_out[l, d_sb:]), d_sb, d_sgu,
                     _pick_tile(n, 512))
        i = l // 2
        if l % 2 == 0:
            xf, h2 = _out_proj(y_sb, y_sgu, xf, w_out[l].astype(BF16), row2(g_ffn[l]), None, 0,
                               _pick_tile(n, 512))
            d_ff = ffn_w_gate.shape[2]
            assert d_ff % FF_CHUNK == 0
            nc = d_ff // FF_CHUNK
            wg = ffn_w_gate[i].astype(BF16).reshape(d, nc, FF_CHUNK).transpose(1, 0, 2)
            wu = ffn_w_up[i].astype(BF16).reshape(d, nc, FF_CHUNK).transpose(1, 0, 2)
            wd = ffn_w_down[i].astype(BF16).reshape(nc, FF_CHUNK, d)
            xf = _ffn(xf, h2, wg, wu, wd, g_fin, last, _pick_tile(n, 512))
        else:
            rw = jnp.pad(router_w[i].astype(F32), ((0, 0), (0, LANES - n_experts)))
            xf, h2, gates = _out_proj(y_sb, y_sgu, xf, w_out[l].astype(BF16), row2(g_ffn[l]), rw,
                                      n_experts, _pick_tile(n, 512))
            xf = _moe(xf, h2, gates, moe_w_gate[i].astype(BF16), moe_w_up[i].astype(BF16),
                      moe_w_down[i].astype(BF16), g_fin, last, _pick_tile(n, 1024))
    return xf.reshape(batch, seq, d)
```

```python
import functools

import jax
import jax.numpy as jnp
from jax import lax
from jax.experimental import pallas as pl
from jax.experimental.pallas import tpu as pltpu

HEAD_DIM = 64
BLK = 128
LANES = 128
EPS = 1e-6
LOG2E = 1.4426950408889634
UNDERFLOW_LOG = -104.0
FF_CHUNK = 256
MOE_ROW_TILE = 512
MOE_TOKEN_TILE = 256
VMEM_LIMIT = 56 * 1024 * 1024

F32 = jnp.float32
BF16 = jnp.bfloat16


def _params(*sem):
    return pltpu.CompilerParams(dimension_semantics=sem, vmem_limit_bytes=VMEM_LIMIT)


def _split_bf16(a):
    hi = a.astype(BF16)
    lo = (a - hi.astype(F32)).astype(BF16)
    return hi, lo


def _same_head_mean_matrix():
    r = lax.broadcasted_iota(jnp.int32, (LANES, LANES), 0)
    c = lax.broadcasted_iota(jnp.int32, (LANES, LANES), 1)
    return jnp.where((r < HEAD_DIM) == (c < HEAD_DIM), 1.0 / HEAD_DIM, 0.0).astype(BF16)


def _head_rms_norm(y, mean_mat, gain):
    hi, lo = _split_bf16(y * y)
    ms = (jnp.dot(hi, mean_mat, preferred_element_type=F32)
          + jnp.dot(lo, mean_mat, preferred_element_type=F32))
    return y * lax.rsqrt(ms + EPS) * gain


def _in_proj_kernel(x_ref, g_ref, cs_ref, w_ref, o_ref, *, n_chunk):
    x = x_ref[...]
    ms = jnp.mean(x * x, axis=-1, keepdims=True)
    h = (x * lax.rsqrt(ms + EPS) * g_ref[...]).astype(BF16)
    for c in range(o_ref.shape[1] // n_chunk):
        sl = slice(c * n_chunk, (c + 1) * n_chunk)
        acc = jnp.dot(h, w_ref[:, sl], preferred_element_type=F32)
        o_ref[:, sl] = (acc * cs_ref[:, sl]).astype(o_ref.dtype)


def _in_proj(x, g, col_scale, w, tm):
    n, d = x.shape
    d_in = w.shape[1]
    return pl.pallas_call(
        functools.partial(_in_proj_kernel, n_chunk=512),
        out_shape=jax.ShapeDtypeStruct((n, d_in), BF16),
        grid=(n // tm,),
        in_specs=[
            pl.BlockSpec((tm, d), lambda i: (i, 0)),
            pl.BlockSpec((1, d), lambda i: (0, 0)),
            pl.BlockSpec((1, d_in), lambda i: (0, 0)),
            pl.BlockSpec((d, d_in), lambda i: (0, 0), pipeline_mode=pl.Buffered(1)),
        ],
        out_specs=pl.BlockSpec((tm, d_in), lambda i: (i, 0)),
        compiler_params=_params("parallel"),
        name="in_proj",
    )(x, g, col_scale, w)


def _attn_kernel(q_ref, k_ref, v_ref, g_ref, o_ref, q2_ref, v0_ref, v1_ref, acc_ref, z_ref, a_ref,
                 *, seq, pairs):
    nq = seq // BLK
    row = lax.broadcasted_iota(jnp.int32, (BLK, BLK), 0)
    col = lax.broadcasted_iota(jnp.int32, (BLK, BLK), 1)
    upper = jnp.where(row > col, 1.0, 0.0).astype(BF16)
    upper2 = jnp.concatenate([upper, upper], axis=0)
    causal = jnp.concatenate([col < row, col < row], axis=0)
    mean_mat = _same_head_mean_matrix()
    lane = lax.broadcasted_iota(jnp.int32, (1, pairs * LANES), 1) % LANES
    m0 = jnp.where(lane < HEAD_DIM, 1.0, 0.0).astype(BF16)
    m1 = jnp.where(lane < HEAD_DIM, 0.0, 1.0).astype(BF16)

    v0_ref[...] = v_ref[...] * m0
    v1_ref[...] = v_ref[...] * m1
    for p in range(pairs):
        ls = slice(p * LANES, (p + 1) * LANES)
        q2_ref[p, :, :BLK, :] = (q_ref[:, ls] * m0[:, ls]).reshape(nq, BLK, LANES)
        q2_ref[p, :, BLK:, :] = (q_ref[:, ls] * m1[:, ls]).reshape(nq, BLK, LANES)

    lanes = [slice(p * LANES, (p + 1) * LANES) for p in range(pairs)]

    def scores(i, j, slot):
        ks = pl.multiple_of(j * BLK, BLK)
        for p in range(pairs):
            z_ref[slot, p] = lax.dot_general(q2_ref[p, i], k_ref[pl.ds(ks, BLK), lanes[p]],
                                             (((1,), (1,)), ((), ())),
                                             preferred_element_type=F32)

    def weighted_values(j):
        ks = pl.multiple_of(j * BLK, BLK)
        for p in range(pairs):
            acc_ref[p] += (
                jnp.dot(a_ref[p, :BLK], v0_ref[pl.ds(ks, BLK), lanes[p]], preferred_element_type=F32)
                + jnp.dot(a_ref[p, BLK:], v1_ref[pl.ds(ks, BLK), lanes[p]], preferred_element_type=F32))

    def weights(zs, carries, mask):
        log_betas, sps, splits = [], [], []
        for z in zs:
            e = jnp.exp2(jnp.abs(z) * (-LOG2E))
            sp = jnp.maximum(z, 0.0) + jnp.log(1.0 + e)
            log_betas.append(z - sp)
            if mask is not None:
                sp = jnp.where(mask, sp, 0.0)
            sps.append(sp)
            splits.append(jnp.concatenate(_split_bf16(sp), axis=1))
        new_carries = []
        for p in range(pairs):
            rowsum = jnp.sum(sps[p], axis=-1, keepdims=True)
            new_carries.append(rowsum if carries is None else carries[p] + rowsum)
        if carries is None:
            alive = jnp.int32(1)
        else:
            low = functools.reduce(jnp.minimum, new_carries)
            alive = (jnp.min(low) < -UNDERFLOW_LOG).astype(jnp.int32)
        betweens = [jnp.dot(s, upper2, preferred_element_type=F32) for s in splits]
        for p in range(pairs):
            between = betweens[p] if carries is None else betweens[p] + carries[p]
            a = jnp.exp(log_betas[p] - between)
            if mask is not None:
                a = jnp.where(mask, a, 0.0)
            a_ref[p] = a.astype(BF16)
        return alive, tuple(new_carries)

    def q_block(i, carry):
        acc_ref[...] = jnp.zeros_like(acc_ref)
        scores(i, i, 0)
        scores(i, jnp.maximum(i - 1, 0), 1)
        alive, carries = weights([z_ref[0, p] for p in range(pairs)], None, causal)

        def more_keys(state):
            return jnp.logical_and(state[0] < i, state[1] > 0)

        def key_block(state):
            jj, cs = state[0], state[2]
            j = i - 1 - jj
            slot = (jj + 1) % 2
            zs = [z_ref[slot, p] for p in range(pairs)]
            weighted_values(j + 1)
            scores(i, jnp.maximum(j - 1, 0), 1 - slot)
            alive, cs = weights(zs, cs, None)
            return jj + 1, alive, cs

        done = lax.while_loop(more_keys, key_block, (jnp.int32(0), alive, carries))[0]
        weighted_values(i - done)
        qs = pl.multiple_of(i * BLK, BLK)
        for p in range(pairs):
            ls = slice(p * LANES, (p + 1) * LANES)
            y = _head_rms_norm(acc_ref[p], mean_mat, g_ref[:, ls])
            o_ref[pl.ds(qs, BLK), ls] = y.astype(o_ref.dtype)
        return carry

    lax.fori_loop(0, nq, q_block, 0)


def _attention(proj, g_sb, batch, seq, d_sb):
    n = proj.shape[0]
    pairs = d_sb // LANES
    blk = lambda off: pl.BlockSpec((seq, d_sb), lambda b: (b, off))
    return pl.pallas_call(
        functools.partial(_attn_kernel, seq=seq, pairs=pairs),
        out_shape=jax.ShapeDtypeStruct((n, d_sb), BF16),
        grid=(batch,),
        in_specs=[blk(0), blk(1), blk(2), pl.BlockSpec((1, d_sb), lambda b: (0, 0))],
        out_specs=pl.BlockSpec((seq, d_sb), lambda b: (b, 0)),
        scratch_shapes=[pltpu.VMEM((pairs, seq // BLK, 2 * BLK, LANES), BF16),
                        pltpu.VMEM((seq, d_sb), BF16),
                        pltpu.VMEM((seq, d_sb), BF16),
                        pltpu.VMEM((pairs, BLK, LANES), F32),
                        pltpu.VMEM((2, pairs, 2 * BLK, LANES), F32),
                        pltpu.VMEM((pairs, 2 * BLK, LANES), BF16)],
        compiler_params=_params("parallel"),
        name="stickbreak_attn",
    )(proj, proj, proj, g_sb)


def _sgu_kernel(u_ref, gt_ref, w_ref, b_ref, gs_ref, go_ref, o_ref):
    tm, d_sgu = o_ref.shape
    row = lax.broadcasted_iota(jnp.int32, (BLK, BLK), 0)
    col = lax.broadcasted_iota(jnp.int32, (BLK, BLK), 1)
    tril = col <= row
    head0 = col < HEAD_DIM
    mean_mat = _same_head_mean_matrix()
    for p in range(d_sgu // LANES):
        ls = slice(p * LANES, (p + 1) * LANES)
        w0 = jnp.where(tril, w_ref[2 * p], 0.0).astype(BF16)
        w1 = jnp.where(tril, w_ref[2 * p + 1], 0.0).astype(BF16)
        for c in range(tm // BLK):
            rs = slice(c * BLK, (c + 1) * BLK)
            gate = jax.nn.gelu(gt_ref[rs, ls].astype(F32))
            gate = _head_rms_norm(gate, mean_mat, gs_ref[:, ls]).astype(BF16)
            mixed = jnp.where(head0,
                              jnp.dot(w0, gate, preferred_element_type=F32),
                              jnp.dot(w1, gate, preferred_element_type=F32)) + b_ref[:, ls]
            y = jax.nn.gelu(u_ref[rs, ls].astype(F32)) * mixed
            o_ref[rs, ls] = _head_rms_norm(y, mean_mat, go_ref[:, ls]).astype(o_ref.dtype)


def _sgu(proj, w_s, bias, g_gate, g_o, d_sb, d_sgu, tm):
    n = proj.shape[0]
    u_blk = 3 * d_sb // d_sgu
    return pl.pallas_call(
        _sgu_kernel,
        out_shape=jax.ShapeDtypeStruct((n, d_sgu), BF16),
        grid=(n // tm,),
        in_specs=[
            pl.BlockSpec((tm, d_sgu), lambda i: (i, u_blk)),
            pl.BlockSpec((tm, d_sgu), lambda i: (i, u_blk + 1)),
            pl.BlockSpec(w_s.shape, lambda i: (0, 0, 0)),
            pl.BlockSpec(bias.shape, lambda i: (0, 0)),
            pl.BlockSpec((1, d_sgu), lambda i: (0, 0)),
            pl.BlockSpec((1, d_sgu), lambda i: (0, 0)),
        ],
        out_specs=pl.BlockSpec((tm, d_sgu), lambda i: (i, 0)),
        compiler_params=_params("parallel"),
        name="spatial_gating",
    )(proj, proj, w_s, bias, g_gate, g_o)


def _top2_route(logits, n_experts):
    lane = lax.broadcasted_iota(jnp.int32, logits.shape, 1).astype(F32)
    neg = jnp.float32(-jnp.inf)
    lg = jnp.where(lane < n_experts, logits, neg)
    m1 = jnp.max(lg, axis=-1, keepdims=True)
    i1 = jnp.min(jnp.where(lg == m1, lane, float(LANES)), axis=-1, keepdims=True)
    lg2 = jnp.where(lane == i1, neg, lg)
    m2 = jnp.max(lg2, axis=-1, keepdims=True)
    i2 = jnp.min(jnp.where(lg2 == m2, lane, float(LANES)), axis=-1, keepdims=True)
    e2 = jnp.exp(m2 - m1)
    w1 = 1.0 / (1.0 + e2)
    w2 = e2 / (1.0 + e2)
    return (jnp.where(lane == 0.0, i1, 0.0) + jnp.where(lane == 1.0, i2, 0.0)
            + jnp.where(lane == 2.0, w1, 0.0) + jnp.where(lane == 3.0, w2, 0.0))


def _out_proj_kernel(*refs, n_experts):
    if n_experts:
        ysb_ref, ysgu_ref, x_ref, w_ref, g_ref, rw_ref, xo_ref, h_ref, gates_ref = refs
    else:
        ysb_ref, ysgu_ref, x_ref, w_ref, g_ref, xo_ref, h_ref = refs
    y = jnp.concatenate([ysb_ref[...], ysgu_ref[...]], axis=1)
    xn = x_ref[...] + jnp.dot(y, w_ref[...], preferred_element_type=F32)
    xo_ref[...] = xn
    ms = jnp.mean(xn * xn, axis=-1, keepdims=True)
    h = xn * lax.rsqrt(ms + EPS) * g_ref[...]
    h_ref[...] = h.astype(h_ref.dtype)
    if n_experts:
        h_hi, h_lo = _split_bf16(h)
        w_hi, w_lo = _split_bf16(rw_ref[...])
        logits = (jnp.dot(h_hi, w_hi, preferred_element_type=F32)
                  + jnp.dot(h_lo, w_hi, preferred_element_type=F32)
                  + jnp.dot(h_hi, w_lo, preferred_element_type=F32))
        gates_ref[...] = _top2_route(logits, n_experts)


def _out_proj(ysb, ysgu, x, w, g, router_w, n_experts, tm):
    n, d = x.shape
    d_half = ysb.shape[1]
    row_blk = lambda width: pl.BlockSpec((tm, width), lambda i: (i, 0))
    const = lambda shape: pl.BlockSpec(shape, lambda i: (0, 0))
    in_specs = [row_blk(d_half), row_blk(ysgu.shape[1]), row_blk(d), const(w.shape), const((1, d))]
    out_shape = [jax.ShapeDtypeStruct((n, d), F32),
                 jax.ShapeDtypeStruct((n, d), F32 if n_experts else BF16)]
    out_specs = [row_blk(d), row_blk(d)]
    args = [ysb, ysgu, x, w, g]
    if n_experts:
        in_specs.append(const(router_w.shape))
        out_shape.append(jax.ShapeDtypeStruct((n, LANES), F32))
        out_specs.append(row_blk(LANES))
        args.append(router_w)
    return pl.pallas_call(
        functools.partial(_out_proj_kernel, n_experts=n_experts),
        out_shape=out_shape,
        grid=(n // tm,),
        in_specs=in_specs,
        out_specs=out_specs,
        compiler_params=_params("parallel"),
        name="out_proj_router" if n_experts else "out_proj",
    )(*args)


def _final_norm(x, gain):
    ms = jnp.mean(x * x, axis=-1, keepdims=True)
    return x * lax.rsqrt(ms + EPS) * gain


def _ffn_kernel(x_ref, h_ref, wg_ref, wu_ref, wd_ref, gf_ref, o_ref, *, final):
    h = h_ref[...]
    o_ref[...] = x_ref[...]

    def chunk(c, carry):
        g = jnp.dot(h, wg_ref[c], preferred_element_type=F32)
        u = jnp.dot(h, wu_ref[c], preferred_element_type=F32)
        act = (jax.nn.silu(g) * u).astype(BF16)
        o_ref[...] += jnp.dot(act, wd_ref[c], preferred_element_type=F32)
        return carry

    lax.fori_loop(0, wg_ref.shape[0], chunk, 0)
    if final:
        o_ref[...] = _final_norm(o_ref[...], gf_ref[...])


def _ffn(x, h, wg, wu, wd, g_final, final, tm):
    n, d = x.shape
    const3 = lambda a: pl.BlockSpec(a.shape, lambda i: (0, 0, 0), pipeline_mode=pl.Buffered(1))
    return pl.pallas_call(
        functools.partial(_ffn_kernel, final=final),
        out_shape=jax.ShapeDtypeStruct((n, d), F32),
        grid=(n // tm,),
        in_specs=[
            pl.BlockSpec((tm, d), lambda i: (i, 0)),
            pl.BlockSpec((tm, d), lambda i: (i, 0)),
            const3(wg), const3(wu), const3(wd),
            pl.BlockSpec((1, d), lambda i: (0, 0)),
        ],
        out_specs=pl.BlockSpec((tm, d), lambda i: (i, 0)),
        compiler_params=_params("parallel"),
        name="dense_swiglu",
    )(x, h, wg, wu, wd, g_final)


def _ff_chunks(d_ff):
    starts = list(range(0, d_ff, FF_CHUNK))
    return [(s, min(FF_CHUNK, d_ff - s)) for s in starts]


class _RowGather:
    def __init__(self, src_hbm, idx_hbm, buf, idx_smem, row_sem, idx_sem):
        self.src, self.idx_hbm, self.buf, self.idx = src_hbm, idx_hbm, buf, idx_smem
        self.row_sem, self.idx_sem = row_sem, idx_sem
        self.rows = buf.shape[1]

    def _idx_copy(self, tile, slot):
        return pltpu.make_async_copy(self.idx_hbm.at[tile], self.idx.at[slot], self.idx_sem.at[slot])

    def _start_rows(self, slot):
        for r in range(self.rows):
            pltpu.make_async_copy(self.src.at[pl.ds(self.idx[slot, r], 1)],
                                  self.buf.at[slot, pl.ds(r, 1)], self.row_sem.at[slot]).start()

    def _wait_rows(self, slot):
        pltpu.make_async_copy(self.src.at[pl.ds(0, self.rows)], self.buf.at[slot],
                              self.row_sem.at[slot]).wait()

    def step(self, compute):
        i = pl.program_id(0)
        last = pl.num_programs(0) - 1
        slot = i % 2
        nxt = jnp.minimum(i + 1, last)
        nxt2 = jnp.minimum(i + 2, last)

        @pl.when(i == 0)
        def _():
            first = self._idx_copy(0, 0)
            first.start()
            first.wait()
            self._start_rows(0)
            self._idx_copy(nxt, 1).start()

        self._wait_rows(slot)
        self._idx_copy(nxt, 1 - slot).wait()
        self._start_rows(1 - slot)
        self._idx_copy(nxt2, slot).start()
        compute(self.buf.at[slot])

        @pl.when(i == last)
        def _():
            self._wait_rows(1 - slot)
            self._idx_copy(nxt2, slot).wait()


def _grouped_ffn_kernel(te_ref, h_hbm, idx_hbm, wg_ref, wu_ref, wd_ref, o_ref,
                        xbuf, idx_smem, row_sem, idx_sem):
    del te_ref

    def compute(x_ref):
        x = x_ref[...].astype(BF16)
        acc = None
        for s, w in _ff_chunks(wg_ref.shape[2]):
            g = jnp.dot(x, wg_ref[0, :, s:s + w], preferred_element_type=F32)
            u = jnp.dot(x, wu_ref[0, :, s:s + w], preferred_element_type=F32)
            act = (jax.nn.silu(g) * u).astype(BF16)
            part = jnp.dot(act, wd_ref[0, s:s + w, :], preferred_element_type=F32)
            acc = part if acc is None else acc + part
        o_ref[...] = acc

    _RowGather(h_hbm, idx_hbm, xbuf, idx_smem, row_sem, idx_sem).step(compute)


def _grouped_ffn(h, row_token, tile_expert, wg, wu, wd, tm):
    n, d = h.shape
    n_tiles = row_token.shape[0]
    d_ff = wg.shape[2]
    grid_spec = pltpu.PrefetchScalarGridSpec(
        num_scalar_prefetch=1,
        grid=(n_tiles,),
        in_specs=[
            pl.BlockSpec(memory_space=pl.ANY),
            pl.BlockSpec(memory_space=pl.ANY),
            pl.BlockSpec((1, d, d_ff), lambda i, te: (te[i], 0, 0)),
            pl.BlockSpec((1, d, d_ff), lambda i, te: (te[i], 0, 0)),
            pl.BlockSpec((1, d_ff, d), lambda i, te: (te[i], 0, 0)),
        ],
        out_specs=pl.BlockSpec((tm, d), lambda i, te: (i, 0)),
        scratch_shapes=[pltpu.VMEM((2, tm, d), F32), pltpu.SMEM((2, tm), jnp.int32),
                        pltpu.SemaphoreType.DMA((2,)), pltpu.SemaphoreType.DMA((2,))],
    )
    return pl.pallas_call(
        _grouped_ffn_kernel,
        out_shape=jax.ShapeDtypeStruct((n_tiles * tm, d), F32),
        grid_spec=grid_spec,
        compiler_params=_params("arbitrary"),
        name="expert_swiglu",
    )(tile_expert, h, row_token, wg, wu, wd)


def _combine_kernel(x_ref, route_ref, gf_ref, y_hbm, idx_hbm, o_ref,
                    ybuf, idx_smem, row_sem, idx_sem, *, final):
    tc = o_ref.shape[0]

    def compute(y_ref):
        route = route_ref[...]
        lane = lax.broadcasted_iota(jnp.int32, route.shape, 1)
        w1 = jnp.sum(jnp.where(lane == 2, route, 0.0), axis=-1, keepdims=True)
        w2 = jnp.sum(jnp.where(lane == 3, route, 0.0), axis=-1, keepdims=True)
        out = x_ref[...] + w1 * y_ref[:tc, :] + w2 * y_ref[tc:, :]
        o_ref[...] = _final_norm(out, gf_ref[...]) if final else out

    _RowGather(y_hbm, idx_hbm, ybuf, idx_smem, row_sem, idx_sem).step(compute)


def _combine(x, route, g_final, y_sorted, dest, final, tc):
    n, d = x.shape
    row = lambda width: pl.BlockSpec((tc, width), lambda i: (i, 0))
    return pl.pallas_call(
        functools.partial(_combine_kernel, final=final),
        out_shape=jax.ShapeDtypeStruct((n, d), F32),
        grid=(n // tc,),
        in_specs=[row(d), row(LANES), pl.BlockSpec((1, d), lambda i: (0, 0)),
                  pl.BlockSpec(memory_space=pl.ANY), pl.BlockSpec(memory_space=pl.ANY)],
        out_specs=row(d),
        scratch_shapes=[pltpu.VMEM((2, 2 * tc, d), F32), pltpu.SMEM((2, 2 * tc), jnp.int32),
                        pltpu.SemaphoreType.DMA((2,)), pltpu.SemaphoreType.DMA((2,))],
        compiler_params=_params("arbitrary"),
        name="expert_combine",
    )(x, route, g_final, y_sorted, dest)


def _routing_tables(route, n_experts, tm, tc):
    n = route.shape[0]
    ids = route[:, :2].astype(jnp.int32)
    onehot = (ids[:, :, None] == jnp.arange(n_experts, dtype=jnp.int32)).any(axis=1).astype(jnp.int32)
    counts = onehot.sum(axis=0)
    group = (counts + tm - 1) // tm * tm
    group_end = jnp.cumsum(group)
    before = jnp.cumsum(onehot, axis=0) - onehot
    dest = (group_end - group)[ids] + jnp.take_along_axis(before, ids, axis=1)
    n_tiles = (2 * n) // tm + n_experts
    token = jnp.broadcast_to(jnp.arange(n, dtype=jnp.int32)[:, None], (n, 2))
    row_token = jnp.zeros((n_tiles * tm,), jnp.int32).at[dest.reshape(-1)].set(token.reshape(-1))
    tile_start = jnp.arange(n_tiles, dtype=jnp.int32) * tm
    tile_expert = jnp.minimum(jnp.searchsorted(group_end, tile_start, side="right"),
                              n_experts - 1).astype(jnp.int32)
    dest_tiles = dest.reshape(n // tc, tc, 2).transpose(0, 2, 1).reshape(n // tc, 2 * tc)
    return row_token.reshape(n_tiles, tm), tile_expert, dest_tiles


def _pick_tile(n, want):
    tm = min(n, want)
    assert n % tm == 0 and tm % BLK == 0, (n, tm)
    return tm


def kernel(x, w_in, w_out, g_mix, g_ffn, g_sgu, sgu_w, sgu_b, g_out, ffn_w_gate, ffn_w_up,
           ffn_w_down, router_w, moe_w_gate, moe_w_up, moe_w_down, g_final):
    batch, seq, d = x.shape
    n = batch * seq
    depth = w_in.shape[0]
    n_groups = sgu_w.shape[1]
    d_sgu = n_groups * HEAD_DIM
    d_in = w_in.shape[2]
    d_sb = (d_in - 2 * d_sgu) // 3
    n_experts = router_w.shape[2]
    assert seq % BLK == 0 and sgu_w.shape[2] == BLK and d_sb % LANES == 0 and d_sgu % LANES == 0
    assert 3 * d_sb % d_sgu == 0 and d % LANES == 0

    col_scale = jnp.where(jnp.arange(d_in) < d_sb, HEAD_DIM ** -0.5, 1.0).astype(F32)[None, :]
    row2 = lambda a: a.reshape(1, -1).astype(F32)
    g_fin = row2(g_final)

    xf = x.reshape(n, d)
    for l in range(depth):
        last = l == depth - 1
        proj = _in_proj(xf, row2(g_mix[l]), col_scale, w_in[l].astype(BF16), _pick_tile(n, 512))
        y_sb = _attention(proj, row2(g_out[l, :d_sb]), batch, seq, d_sb)
        bias = jnp.repeat(sgu_b[l].T, HEAD_DIM, axis=1)
        y_sgu = _sgu(proj, sgu_w[l], bias, row2(g_sgu[l]), row2(g_out[l, d_sb:]), d_sb, d_sgu,
                     _pick_tile(n, 512))
        i = l // 2
        if l % 2 == 0:
            xf, h2 = _out_proj(y_sb, y_sgu, xf, w_out[l].astype(BF16), row2(g_ffn[l]), None, 0,
                               _pick_tile(n, 512))
            d_ff = ffn_w_gate.shape[2]
            assert d_ff % FF_CHUNK == 0
            nc = d_ff // FF_CHUNK
            wg = ffn_w_gate[i].astype(BF16).reshape(d, nc, FF_CHUNK).transpose(1, 0, 2)
            wu = ffn_w_up[i].astype(BF16).reshape(d, nc, FF_CHUNK).transpose(1, 0, 2)
            wd = ffn_w_down[i].astype(BF16).reshape(nc, FF_CHUNK, d)
            xf = _ffn(xf, h2, wg, wu, wd, g_fin, last, _pick_tile(n, 512))
        else:
            rw = jnp.pad(router_w[i].astype(F32), ((0, 0), (0, LANES - n_experts)))
            xf, h2, route = _out_proj(y_sb, y_sgu, xf, w_out[l].astype(BF16), row2(g_ffn[l]), rw,
                                      n_experts, _pick_tile(n, 512))
            tm, tc = _pick_tile(2 * n, MOE_ROW_TILE), _pick_tile(n, MOE_TOKEN_TILE)
            row_token, tile_expert, dest = _routing_tables(route, n_experts, tm, tc)
            y_sorted = _grouped_ffn(h2, row_token, tile_expert, moe_w_gate[i].astype(BF16),
                                    moe_w_up[i].astype(BF16), moe_w_down[i].astype(BF16), tm)
            xf = _combine(xf, route, g_fin, y_sorted, dest, last, tc)
    return xf.reshape(batch, seq, d)
```

```python
import functools

import jax
import jax.numpy as jnp
from jax import lax
from jax.experimental import pallas as pl
from jax.experimental.pallas import tpu as pltpu

HEAD_DIM = 64
BLK = 128
LANES = 128
SUBLANES = 8
EPS = 1e-6
LOG2E = 1.4426950408889634
UNDERFLOW_LOG = -104.0
FF_CHUNK = 256
MOE_ROW_TILE = 512
MOE_TOKEN_TILE = 256
VMEM_LIMIT = 56 * 1024 * 1024

F32 = jnp.float32
BF16 = jnp.bfloat16


def _params(*sem):
    return pltpu.CompilerParams(dimension_semantics=sem, vmem_limit_bytes=VMEM_LIMIT)


def _split_bf16(a):
    hi = a.astype(BF16)
    lo = (a - hi.astype(F32)).astype(BF16)
    return hi, lo


def _same_head_mean_matrix():
    r = lax.broadcasted_iota(jnp.int32, (LANES, LANES), 0)
    c = lax.broadcasted_iota(jnp.int32, (LANES, LANES), 1)
    return jnp.where((r < HEAD_DIM) == (c < HEAD_DIM), 1.0 / HEAD_DIM, 0.0).astype(BF16)


def _head_rms_norm(y, mean_mat, gain):
    hi, lo = _split_bf16(y * y)
    ms = (jnp.dot(hi, mean_mat, preferred_element_type=F32)
          + jnp.dot(lo, mean_mat, preferred_element_type=F32))
    return y * lax.rsqrt(ms + EPS) * gain


def _in_proj_kernel(x_ref, g_ref, cs_ref, w_ref, o_ref, *, n_chunk):
    x = x_ref[...]
    ms = jnp.mean(x * x, axis=-1, keepdims=True)
    h = (x * lax.rsqrt(ms + EPS) * g_ref[...]).astype(BF16)
    for c in range(o_ref.shape[1] // n_chunk):
        sl = slice(c * n_chunk, (c + 1) * n_chunk)
        acc = jnp.dot(h, w_ref[:, sl], preferred_element_type=F32)
        o_ref[:, sl] = (acc * cs_ref[:, sl]).astype(o_ref.dtype)


def _in_proj(x, g, col_scale, w, tm):
    n, d = x.shape
    d_in = w.shape[1]
    return pl.pallas_call(
        functools.partial(_in_proj_kernel, n_chunk=512),
        out_shape=jax.ShapeDtypeStruct((n, d_in), BF16),
        grid=(n // tm,),
        in_specs=[
            pl.BlockSpec((tm, d), lambda i: (i, 0)),
            pl.BlockSpec((1, d), lambda i: (0, 0)),
            pl.BlockSpec((1, d_in), lambda i: (0, 0)),
            pl.BlockSpec((d, d_in), lambda i: (0, 0), pipeline_mode=pl.Buffered(1)),
        ],
        out_specs=pl.BlockSpec((tm, d_in), lambda i: (i, 0)),
        compiler_params=_params("parallel"),
        name="in_proj",
    )(x, g, col_scale, w)


def _attn_kernel(q_ref, k_ref, v_ref, g_ref, o_ref, q2_ref, v0_ref, v1_ref, acc_ref, z_ref, a_ref,
                 *, seq, pairs):
    nq = seq // BLK
    row = lax.broadcasted_iota(jnp.int32, (BLK, BLK), 0)
    col = lax.broadcasted_iota(jnp.int32, (BLK, BLK), 1)
    upper = jnp.where(row > col, 1.0, 0.0).astype(BF16)
    upper2 = jnp.concatenate([upper, upper], axis=0)
    causal = jnp.concatenate([col < row, col < row], axis=0)
    mean_mat = _same_head_mean_matrix()
    lane = lax.broadcasted_iota(jnp.int32, (1, pairs * LANES), 1) % LANES
    m0 = jnp.where(lane < HEAD_DIM, 1.0, 0.0).astype(BF16)
    m1 = jnp.where(lane < HEAD_DIM, 0.0, 1.0).astype(BF16)

    v0_ref[...] = v_ref[...] * m0
    v1_ref[...] = v_ref[...] * m1
    for p in range(pairs):
        ls = slice(p * LANES, (p + 1) * LANES)
        q2_ref[p, :, :BLK, :] = (q_ref[:, ls] * m0[:, ls]).reshape(nq, BLK, LANES)
        q2_ref[p, :, BLK:, :] = (q_ref[:, ls] * m1[:, ls]).reshape(nq, BLK, LANES)

    lanes = [slice(p * LANES, (p + 1) * LANES) for p in range(pairs)]

    def scores(i, j, slot):
        ks = pl.multiple_of(j * BLK, BLK)
        for p in range(pairs):
            z_ref[slot, p] = lax.dot_general(q2_ref[p, i], k_ref[pl.ds(ks, BLK), lanes[p]],
                                             (((1,), (1,)), ((), ())),
                                             preferred_element_type=F32)

    def weighted_values(j):
        ks = pl.multiple_of(j * BLK, BLK)
        for p in range(pairs):
            acc_ref[p] += (
                jnp.dot(a_ref[p, :BLK], v0_ref[pl.ds(ks, BLK), lanes[p]], preferred_element_type=F32)
                + jnp.dot(a_ref[p, BLK:], v1_ref[pl.ds(ks, BLK), lanes[p]], preferred_element_type=F32))

    def weights(zs, carries, mask):
        log_betas, sps, splits = [], [], []
        for z in zs:
            e = jnp.exp2(jnp.abs(z) * (-LOG2E))
            sp = jnp.maximum(z, 0.0) + jnp.log(1.0 + e)
            log_betas.append(z - sp)
            if mask is not None:
                sp = jnp.where(mask, sp, 0.0)
            sps.append(sp)
            splits.append(jnp.concatenate(_split_bf16(sp), axis=1))
        new_carries = []
        for p in range(pairs):
            rowsum = jnp.sum(sps[p], axis=-1, keepdims=True)
            new_carries.append(rowsum if carries is None else carries[p] + rowsum)
        if carries is None:
            alive = jnp.int32(1)
        else:
            low = functools.reduce(jnp.minimum, new_carries)
            alive = (jnp.min(low) < -UNDERFLOW_LOG).astype(jnp.int32)
        betweens = [jnp.dot(s, upper2, preferred_element_type=F32) for s in splits]
        for p in range(pairs):
            between = betweens[p] if carries is None else betweens[p] + carries[p]
            a = jnp.exp(log_betas[p] - between)
            if mask is not None:
                a = jnp.where(mask, a, 0.0)
            a_ref[p] = a.astype(BF16)
        return alive, tuple(new_carries)

    def q_block(i, carry):
        acc_ref[...] = jnp.zeros_like(acc_ref)
        scores(i, i, 0)
        scores(i, jnp.maximum(i - 1, 0), 1)
        alive, carries = weights([z_ref[0, p] for p in range(pairs)], None, causal)

        def more_keys(state):
            return jnp.logical_and(state[0] < i, state[1] > 0)

        def key_block(state):
            jj, cs = state[0], state[2]
            j = i - 1 - jj
            slot = (jj + 1) % 2
            zs = [z_ref[slot, p] for p in range(pairs)]
            weighted_values(j + 1)
            scores(i, jnp.maximum(j - 1, 0), 1 - slot)
            alive, cs = weights(zs, cs, None)
            return jj + 1, alive, cs

        done = lax.while_loop(more_keys, key_block, (jnp.int32(0), alive, carries))[0]
        weighted_values(i - done)
        qs = pl.multiple_of(i * BLK, BLK)
        for p in range(pairs):
            ls = slice(p * LANES, (p + 1) * LANES)
            y = _head_rms_norm(acc_ref[p], mean_mat, g_ref[:, ls])
            o_ref[pl.ds(qs, BLK), ls] = y.astype(o_ref.dtype)
        return carry

    lax.fori_loop(0, nq, q_block, 0)


def _attention(proj, g_sb, batch, seq, d_sb):
    n = proj.shape[0]
    pairs = d_sb // LANES
    blk = lambda off: pl.BlockSpec((seq, d_sb), lambda b: (b, off))
    return pl.pallas_call(
        functools.partial(_attn_kernel, seq=seq, pairs=pairs),
        out_shape=jax.ShapeDtypeStruct((n, d_sb), BF16),
        grid=(batch,),
        in_specs=[blk(0), blk(1), blk(2), pl.BlockSpec((1, d_sb), lambda b: (0, 0))],
        out_specs=pl.BlockSpec((seq, d_sb), lambda b: (b, 0)),
        scratch_shapes=[pltpu.VMEM((pairs, seq // BLK, 2 * BLK, LANES), BF16),
                        pltpu.VMEM((seq, d_sb), BF16),
                        pltpu.VMEM((seq, d_sb), BF16),
                        pltpu.VMEM((pairs, BLK, LANES), F32),
                        pltpu.VMEM((2, pairs, 2 * BLK, LANES), F32),
                        pltpu.VMEM((pairs, 2 * BLK, LANES), BF16)],
        compiler_params=_params("parallel"),
        name="stickbreak_attn",
    )(proj, proj, proj, g_sb)


def _sgu_kernel(u_ref, gt_ref, w_ref, b_ref, gs_ref, go_ref, o_ref):
    tm, d_sgu = o_ref.shape
    row = lax.broadcasted_iota(jnp.int32, (BLK, BLK), 0)
    col = lax.broadcasted_iota(jnp.int32, (BLK, BLK), 1)
    tril = col <= row
    head0 = col < HEAD_DIM
    mean_mat = _same_head_mean_matrix()
    for p in range(d_sgu // LANES):
        ls = slice(p * LANES, (p + 1) * LANES)
        w0 = jnp.where(tril, w_ref[2 * p], 0.0).astype(BF16)
        w1 = jnp.where(tril, w_ref[2 * p + 1], 0.0).astype(BF16)
        for c in range(tm // BLK):
            rs = slice(c * BLK, (c + 1) * BLK)
            gate = jax.nn.gelu(gt_ref[rs, ls].astype(F32))
            gate = _head_rms_norm(gate, mean_mat, gs_ref[:, ls]).astype(BF16)
            mixed = jnp.where(head0,
                              jnp.dot(w0, gate, preferred_element_type=F32),
                              jnp.dot(w1, gate, preferred_element_type=F32)) + b_ref[:, ls]
            y = jax.nn.gelu(u_ref[rs, ls].astype(F32)) * mixed
            o_ref[rs, ls] = _head_rms_norm(y, mean_mat, go_ref[:, ls]).astype(o_ref.dtype)


def _sgu(proj, w_s, bias, g_gate, g_o, d_sb, d_sgu, tm):
    n = proj.shape[0]
    u_blk = 3 * d_sb // d_sgu
    return pl.pallas_call(
        _sgu_kernel,
        out_shape=jax.ShapeDtypeStruct((n, d_sgu), BF16),
        grid=(n // tm,),
        in_specs=[
            pl.BlockSpec((tm, d_sgu), lambda i: (i, u_blk)),
            pl.BlockSpec((tm, d_sgu), lambda i: (i, u_blk + 1)),
            pl.BlockSpec(w_s.shape, lambda i: (0, 0, 0)),
            pl.BlockSpec(bias.shape, lambda i: (0, 0)),
            pl.BlockSpec((1, d_sgu), lambda i: (0, 0)),
            pl.BlockSpec((1, d_sgu), lambda i: (0, 0)),
        ],
        out_specs=pl.BlockSpec((tm, d_sgu), lambda i: (i, 0)),
        compiler_params=_params("parallel"),
        name="spatial_gating",
    )(proj, proj, w_s, bias, g_gate, g_o)


def _top2_route(logits, n_experts):
    lane = lax.broadcasted_iota(jnp.int32, logits.shape, 1).astype(F32)
    neg = jnp.float32(-jnp.inf)
    lg = jnp.where(lane < n_experts, logits, neg)
    m1 = jnp.max(lg, axis=-1, keepdims=True)
    i1 = jnp.min(jnp.where(lg == m1, lane, float(LANES)), axis=-1, keepdims=True)
    lg2 = jnp.where(lane == i1, neg, lg)
    m2 = jnp.max(lg2, axis=-1, keepdims=True)
    i2 = jnp.min(jnp.where(lg2 == m2, lane, float(LANES)), axis=-1, keepdims=True)
    e2 = jnp.exp(m2 - m1)
    w1 = 1.0 / (1.0 + e2)
    w2 = e2 / (1.0 + e2)
    return (jnp.where(lane == 0.0, i1, 0.0) + jnp.where(lane == 1.0, i2, 0.0)
            + jnp.where(lane == 2.0, w1, 0.0) + jnp.where(lane == 3.0, w2, 0.0))


def _out_proj_kernel(*refs, n_experts):
    if n_experts:
        ysb_ref, ysgu_ref, x_ref, w_ref, g_ref, rw_ref, xo_ref, h_ref, gates_ref = refs
    else:
        ysb_ref, ysgu_ref, x_ref, w_ref, g_ref, xo_ref, h_ref = refs
    y = jnp.concatenate([ysb_ref[...], ysgu_ref[...]], axis=1)
    xn = x_ref[...] + jnp.dot(y, w_ref[...], preferred_element_type=F32)
    xo_ref[...] = xn
    ms = jnp.mean(xn * xn, axis=-1, keepdims=True)
    h = xn * lax.rsqrt(ms + EPS) * g_ref[...]
    if n_experts:
        _store_token_tiles(h_ref, h)
    else:
        h_ref[...] = h.astype(h_ref.dtype)
    if n_experts:
        h_hi, h_lo = _split_bf16(h)
        w_hi, w_lo = _split_bf16(rw_ref[...])
        logits = (jnp.dot(h_hi, w_hi, preferred_element_type=F32)
                  + jnp.dot(h_lo, w_hi, preferred_element_type=F32)
                  + jnp.dot(h_hi, w_lo, preferred_element_type=F32))
        gates_ref[...] = _top2_route(logits, n_experts)


def _out_proj(ysb, ysgu, x, w, g, router_w, n_experts, tm):
    n, d = x.shape
    d_half = ysb.shape[1]
    row_blk = lambda width: pl.BlockSpec((tm, width), lambda i: (i, 0))
    const = lambda shape: pl.BlockSpec(shape, lambda i: (0, 0))
    in_specs = [row_blk(d_half), row_blk(ysgu.shape[1]), row_blk(d), const(w.shape), const((1, d))]
    if n_experts:
        assert d == SUBLANES * LANES
        out_shape = [jax.ShapeDtypeStruct((n, d), F32),
                     jax.ShapeDtypeStruct((n * SUBLANES, LANES), F32)]
        out_specs = [row_blk(d), pl.BlockSpec((tm * SUBLANES, LANES), lambda i: (i, 0))]
    else:
        out_shape = [jax.ShapeDtypeStruct((n, d), F32), jax.ShapeDtypeStruct((n, d), BF16)]
        out_specs = [row_blk(d), row_blk(d)]
    args = [ysb, ysgu, x, w, g]
    if n_experts:
        in_specs.append(const(router_w.shape))
        out_shape.append(jax.ShapeDtypeStruct((n, LANES), F32))
        out_specs.append(row_blk(LANES))
        args.append(router_w)
    return pl.pallas_call(
        functools.partial(_out_proj_kernel, n_experts=n_experts),
        out_shape=out_shape,
        grid=(n // tm,),
        in_specs=in_specs,
        out_specs=out_specs,
        compiler_params=_params("parallel"),
        name="out_proj_router" if n_experts else "out_proj",
    )(*args)


def _final_norm(x, gain):
    ms = jnp.mean(x * x, axis=-1, keepdims=True)
    return x * lax.rsqrt(ms + EPS) * gain


def _ffn_kernel(x_ref, h_ref, wg_ref, wu_ref, wd_ref, gf_ref, o_ref, *, final):
    h = h_ref[...]
    o_ref[...] = x_ref[...]

    def chunk(c, carry):
        g = jnp.dot(h, wg_ref[c], preferred_element_type=F32)
        u = jnp.dot(h, wu_ref[c], preferred_element_type=F32)
        act = (jax.nn.silu(g) * u).astype(BF16)
        o_ref[...] += jnp.dot(act, wd_ref[c], preferred_element_type=F32)
        return carry

    lax.fori_loop(0, wg_ref.shape[0], chunk, 0)
    if final:
        o_ref[...] = _final_norm(o_ref[...], gf_ref[...])


def _ffn(x, h, wg, wu, wd, g_final, final, tm):
    n, d = x.shape
    const3 = lambda a: pl.BlockSpec(a.shape, lambda i: (0, 0, 0), pipeline_mode=pl.Buffered(1))
    return pl.pallas_call(
        functools.partial(_ffn_kernel, final=final),
        out_shape=jax.ShapeDtypeStruct((n, d), F32),
        grid=(n // tm,),
        in_specs=[
            pl.BlockSpec((tm, d), lambda i: (i, 0)),
            pl.BlockSpec((tm, d), lambda i: (i, 0)),
            const3(wg), const3(wu), const3(wd),
            pl.BlockSpec((1, d), lambda i: (0, 0)),
        ],
        out_specs=pl.BlockSpec((tm, d), lambda i: (i, 0)),
        compiler_params=_params("parallel"),
        name="dense_swiglu",
    )(x, h, wg, wu, wd, g_final)


def _ff_chunks(d_ff):
    starts = list(range(0, d_ff, FF_CHUNK))
    return [(s, min(FF_CHUNK, d_ff - s)) for s in starts]


def _store_token_tiles(ref, value):
    rows = value.shape[0]
    for k in range(SUBLANES):
        ref[pl.ds(k, rows, stride=SUBLANES), :] = value[:, k * LANES:(k + 1) * LANES]


def _load_token_tiles(ref, first, rows):
    return jnp.concatenate([ref[pl.ds(first * SUBLANES + k, rows, stride=SUBLANES), :]
                            for k in range(SUBLANES)], axis=1)


def _dispatch_kernel(fill_ref, h_ref, idx_hbm, xs_hbm, idx_smem, zero_buf, row_sem, idx_sem,
                     fill_sem):
    i = pl.program_id(0)
    last = pl.num_programs(0) - 1
    slot = i % 2
    nxt = jnp.minimum(i + 1, last)
    tc = h_ref.shape[0] // SUBLANES

    def idx_copy(tile, s):
        return pltpu.make_async_copy(idx_hbm.at[tile], idx_smem.at[s], idx_sem.at[s])

    @pl.when(i == 0)
    def _():
        idx_copy(0, 0).start()
        zero_buf[...] = jnp.zeros_like(zero_buf)
        for j in range(fill_ref.shape[0]):
            first_row = pl.multiple_of(fill_ref[j] * SUBLANES, SUBLANES)
            fill = pltpu.make_async_copy(zero_buf, xs_hbm.at[pl.ds(first_row, zero_buf.shape[0])],
                                         fill_sem)
            fill.start()
            fill.wait()

    idx_copy(i, slot).wait()
    idx_copy(nxt, 1 - slot).start()
    for r in range(tc):
        for k in range(2):
            row = pl.multiple_of(idx_smem[slot, k * tc + r] * SUBLANES, SUBLANES)
            pltpu.make_async_copy(h_ref.at[pl.ds(r * SUBLANES, SUBLANES)],
                                  xs_hbm.at[pl.ds(row, SUBLANES)], row_sem).start(priority=k)
    for _ in range(2):
        pltpu.make_async_copy(h_ref, xs_hbm.at[pl.ds(0, tc * SUBLANES)], row_sem).wait()

    @pl.when(i == last)
    def _():
        idx_copy(nxt, 1 - slot).wait()


def _dispatch(h_tiles, dest, fill_start, n_rows, tm, tc):
    n_tok = h_tiles.shape[0] // SUBLANES
    grid_spec = pltpu.PrefetchScalarGridSpec(
        num_scalar_prefetch=1,
        grid=(n_tok // tc,),
        in_specs=[pl.BlockSpec((tc * SUBLANES, LANES), lambda i, fill: (i, 0)),
                  pl.BlockSpec(memory_space=pl.ANY)],
        out_specs=pl.BlockSpec(memory_space=pl.ANY),
        scratch_shapes=[pltpu.SMEM((2, 2 * tc), jnp.int32),
                        pltpu.VMEM((tm * SUBLANES, LANES), F32),
                        pltpu.SemaphoreType.DMA, pltpu.SemaphoreType.DMA((2,)),
                        pltpu.SemaphoreType.DMA],
    )
    return pl.pallas_call(
        _dispatch_kernel,
        out_shape=jax.ShapeDtypeStruct((n_rows * SUBLANES, LANES), F32),
        grid_spec=grid_spec,
        compiler_params=_params("arbitrary"),
        name="expert_dispatch",
    )(fill_start, h_tiles, dest)


def _grouped_ffn_kernel(te_ref, xs_ref, wg_ref, wu_ref, wd_ref, y_ref):
    del te_ref
    tm = xs_ref.shape[0] // SUBLANES
    x = _load_token_tiles(xs_ref, 0, tm).astype(BF16)
    acc = None
    for s, w in _ff_chunks(wg_ref.shape[2]):
        g = jnp.dot(x, wg_ref[0, :, s:s + w], preferred_element_type=F32)
        u = jnp.dot(x, wu_ref[0, :, s:s + w], preferred_element_type=F32)
        act = (jax.nn.silu(g) * u).astype(BF16)
        part = jnp.dot(act, wd_ref[0, s:s + w, :], preferred_element_type=F32)
        acc = part if acc is None else acc + part
    _store_token_tiles(y_ref, acc)


def _grouped_ffn(xs, tile_expert, wg, wu, wd, tm):
    n_tiles = tile_expert.shape[0]
    _, d, d_ff = wg.shape
    rows = pl.BlockSpec((tm * SUBLANES, LANES), lambda i, te: (i, 0))
    grid_spec = pltpu.PrefetchScalarGridSpec(
        num_scalar_prefetch=1,
        grid=(n_tiles,),
        in_specs=[
            rows,
            pl.BlockSpec((1, d, d_ff), lambda i, te: (te[i], 0, 0)),
            pl.BlockSpec((1, d, d_ff), lambda i, te: (te[i], 0, 0)),
            pl.BlockSpec((1, d_ff, d), lambda i, te: (te[i], 0, 0)),
        ],
        out_specs=rows,
    )
    return pl.pallas_call(
        _grouped_ffn_kernel,
        out_shape=jax.ShapeDtypeStruct(xs.shape, F32),
        grid_spec=grid_spec,
        compiler_params=_params("arbitrary"),
        name="expert_swiglu",
    )(tile_expert, xs, wg, wu, wd)


def _combine_kernel(x_ref, route_ref, gf_ref, y_hbm, idx_hbm, o_ref,
                    ybuf, idx_smem, row_sem, idx_sem, *, final):
    tc = o_ref.shape[0]
    n_rows = 2 * tc
    i = pl.program_id(0)
    last = pl.num_programs(0) - 1
    slot = i % 2
    nxt = jnp.minimum(i + 1, last)
    nxt2 = jnp.minimum(i + 2, last)

    def idx_copy(tile, s):
        return pltpu.make_async_copy(idx_hbm.at[tile], idx_smem.at[s], idx_sem.at[s])

    def start_rows(s):
        for r in range(n_rows):
            row = pl.multiple_of(idx_smem[s, r] * SUBLANES, SUBLANES)
            pltpu.make_async_copy(y_hbm.at[pl.ds(row, SUBLANES)],
                                  ybuf.at[s, pl.ds(r * SUBLANES, SUBLANES)],
                                  row_sem.at[s]).start(priority=r % 2)

    def wait_rows(s):
        pltpu.make_async_copy(y_hbm.at[pl.ds(0, n_rows * SUBLANES)], ybuf.at[s],
                              row_sem.at[s]).wait()

    @pl.when(i == 0)
    def _():
        first = idx_copy(0, 0)
        first.start()
        first.wait()
        start_rows(0)
        idx_copy(nxt, 1).start()

    wait_rows(slot)
    idx_copy(nxt, 1 - slot).wait()
    start_rows(1 - slot)
    idx_copy(nxt2, slot).start()

    route = route_ref[...]
    lane = lax.broadcasted_iota(jnp.int32, route.shape, 1)
    w1 = jnp.sum(jnp.where(lane == 2, route, 0.0), axis=-1, keepdims=True)
    w2 = jnp.sum(jnp.where(lane == 3, route, 0.0), axis=-1, keepdims=True)
    y_ref = ybuf.at[slot]
    out = x_ref[...] + w1 * _load_token_tiles(y_ref, 0, tc) + w2 * _load_token_tiles(y_ref, tc, tc)
    o_ref[...] = _final_norm(out, gf_ref[...]) if final else out

    @pl.when(i == last)
    def _():
        wait_rows(1 - slot)
        idx_copy(nxt2, slot).wait()


def _combine(x, route, g_final, y_sorted, dest, final, tc):
    n, d = x.shape
    row = lambda width: pl.BlockSpec((tc, width), lambda i: (i, 0))
    return pl.pallas_call(
        functools.partial(_combine_kernel, final=final),
        out_shape=jax.ShapeDtypeStruct((n, d), F32),
        grid=(n // tc,),
        in_specs=[row(d), row(LANES), pl.BlockSpec((1, d), lambda i: (0, 0)),
                  pl.BlockSpec(memory_space=pl.ANY), pl.BlockSpec(memory_space=pl.ANY)],
        out_specs=row(d),
        scratch_shapes=[pltpu.VMEM((2, 2 * tc * SUBLANES, LANES), F32),
                        pltpu.SMEM((2, 2 * tc), jnp.int32),
                        pltpu.SemaphoreType.DMA((2,)), pltpu.SemaphoreType.DMA((2,))],
        compiler_params=_params("arbitrary"),
        name="expert_combine",
    )(x, route, g_final, y_sorted, dest)


def _routing_tables(route, n_experts, tm, tc):
    n = route.shape[0]
    ids = route[:, :2].astype(jnp.int32)
    onehot = (ids[:, :, None] == jnp.arange(n_experts, dtype=jnp.int32)).any(axis=1).astype(jnp.int32)
    counts = onehot.sum(axis=0)
    group = (counts + tm - 1) // tm * tm
    group_end = jnp.cumsum(group)
    before = jnp.cumsum(onehot, axis=0) - onehot
    dest = (group_end - group)[ids] + jnp.take_along_axis(before, ids, axis=1)
    n_tiles = (2 * n) // tm + n_experts
    tile_start = jnp.arange(n_tiles, dtype=jnp.int32) * tm
    tile_expert = jnp.minimum(jnp.searchsorted(group_end, tile_start, side="right"),
                              n_experts - 1).astype(jnp.int32)
    dest_tiles = dest.reshape(n // tc, tc, 2).transpose(0, 2, 1).reshape(n // tc, 2 * tc)
    n_rows = n_tiles * tm
    pad_windows = jnp.minimum(group_end - group + counts, n_rows - tm)
    tail_windows = n_rows - tm * jnp.arange(1, n_experts + 1, dtype=jnp.int32)
    fill_start = jnp.concatenate([pad_windows, tail_windows]).astype(jnp.int32)
    return n_rows, tile_expert, dest_tiles, fill_start


def _pick_tile(n, want):
    tm = min(n, want)
    assert n % tm == 0 and tm % BLK == 0, (n, tm)
    return tm


def kernel(x, w_in, w_out, g_mix, g_ffn, g_sgu, sgu_w, sgu_b, g_out, ffn_w_gate, ffn_w_up,
           ffn_w_down, router_w, moe_w_gate, moe_w_up, moe_w_down, g_final):
    batch, seq, d = x.shape
    n = batch * seq
    depth = w_in.shape[0]
    n_groups = sgu_w.shape[1]
    d_sgu = n_groups * HEAD_DIM
    d_in = w_in.shape[2]
    d_sb = (d_in - 2 * d_sgu) // 3
    n_experts = router_w.shape[2]
    assert seq % BLK == 0 and sgu_w.shape[2] == BLK and d_sb % LANES == 0 and d_sgu % LANES == 0
    assert 3 * d_sb % d_sgu == 0 and d % LANES == 0

    col_scale = jnp.where(jnp.arange(d_in) < d_sb, HEAD_DIM ** -0.5, 1.0).astype(F32)[None, :]
    row2 = lambda a: a.reshape(1, -1).astype(F32)
    g_fin = row2(g_final)

    xf = x.reshape(n, d)
    for l in range(depth):
        last = l == depth - 1
        proj = _in_proj(xf, row2(g_mix[l]), col_scale, w_in[l].astype(BF16), _pick_tile(n, 512))
        y_sb = _attention(proj, row2(g_out[l, :d_sb]), batch, seq, d_sb)
        bias = jnp.repeat(sgu_b[l].T, HEAD_DIM, axis=1)
        y_sgu = _sgu(proj, sgu_w[l], bias, row2(g_sgu[l]), row2(g_out[l, d_sb:]), d_sb, d_sgu,
                     _pick_tile(n, 512))
        i = l // 2
        if l % 2 == 0:
            xf, h2 = _out_proj(y_sb, y_sgu, xf, w_out[l].astype(BF16), row2(g_ffn[l]), None, 0,
                               _pick_tile(n, 512))
            d_ff = ffn_w_gate.shape[2]
            assert d_ff % FF_CHUNK == 0
            nc = d_ff // FF_CHUNK
            wg = ffn_w_gate[i].astype(BF16).reshape(d, nc, FF_CHUNK).transpose(1, 0, 2)
            wu = ffn_w_up[i].astype(BF16).reshape(d, nc, FF_CHUNK).transpose(1, 0, 2)
            wd = ffn_w_down[i].astype(BF16).reshape(nc, FF_CHUNK, d)
            xf = _ffn(xf, h2, wg, wu, wd, g_fin, last, _pick_tile(n, 512))
        else:
            rw = jnp.pad(router_w[i].astype(F32), ((0, 0), (0, LANES - n_experts)))
            xf, h2, route = _out_proj(y_sb, y_sgu, xf, w_out[l].astype(BF16), row2(g_ffn[l]), rw,
                                      n_experts, _pick_tile(n, 512))
            tm, tc = _pick_tile(2 * n, MOE_ROW_TILE), _pick_tile(n, MOE_TOKEN_TILE)
            n_rows, tile_expert, dest, fill_start = _routing_tables(route, n_experts, tm, tc)
            x_sorted = _dispatch(h2, dest, fill_start, n_rows, tm, tc)
            y_sorted = _grouped_ffn(x_sorted, tile_expert, moe_w_gate[i].astype(BF16),
                                    moe_w_up[i].astype(BF16), moe_w_down[i].astype(BF16), tm)
            xf = _combine(xf, route, g_fin, y_sorted, dest, last, tc)
    return xf.reshape(batch, seq, d)
```

```python
import functools

import jax
import jax.numpy as jnp
from jax import lax
from jax.experimental import pallas as pl
from jax.experimental.pallas import tpu as pltpu

HEAD_DIM = 64
BLK = 128
LANES = 128
SUBLANES = 8
EPS = 1e-6
LOG2E = 1.4426950408889634
UNDERFLOW_LOG = -104.0
FF_CHUNK = 256
MOE_ROW_TILE = 512
MOE_TOKEN_TILE = 256
VMEM_LIMIT = 56 * 1024 * 1024

F32 = jnp.float32
BF16 = jnp.bfloat16


def _params(*sem):
    return pltpu.CompilerParams(dimension_semantics=sem, vmem_limit_bytes=VMEM_LIMIT)


def _split_bf16(a):
    hi = a.astype(BF16)
    lo = (a - hi.astype(F32)).astype(BF16)
    return hi, lo


def _same_head_mean_matrix():
    r = lax.broadcasted_iota(jnp.int32, (LANES, LANES), 0)
    c = lax.broadcasted_iota(jnp.int32, (LANES, LANES), 1)
    return jnp.where((r < HEAD_DIM) == (c < HEAD_DIM), 1.0 / HEAD_DIM, 0.0).astype(BF16)


def _head_rms_norm_many(ys, mean_mat, gain):
    mean2 = jnp.concatenate([mean_mat, mean_mat], axis=0)
    splits = [jnp.concatenate(_split_bf16(y * y), axis=1) for y in ys]
    means = [jnp.dot(s, mean2, preferred_element_type=F32) for s in splits]
    gains = gain if isinstance(gain, (list, tuple)) else [gain] * len(ys)
    return [y * lax.rsqrt(ms + EPS) * g for y, ms, g in zip(ys, means, gains)]


def _head_rms_norm(y, mean_mat, gain):
    return _head_rms_norm_many([y], mean_mat, gain)[0]


def _in_proj_kernel(x_ref, g_ref, cs_ref, w_ref, o_ref, *, n_chunk):
    x = x_ref[...]
    ms = jnp.mean(x * x, axis=-1, keepdims=True)
    h = (x * lax.rsqrt(ms + EPS) * g_ref[...]).astype(BF16)
    for c in range(o_ref.shape[1] // n_chunk):
        sl = slice(c * n_chunk, (c + 1) * n_chunk)
        acc = jnp.dot(h, w_ref[:, sl], preferred_element_type=F32)
        o_ref[:, sl] = (acc * cs_ref[:, sl]).astype(o_ref.dtype)


def _in_proj(x, g, col_scale, w, tm):
    n, d = x.shape
    d_in = w.shape[1]
    return pl.pallas_call(
        functools.partial(_in_proj_kernel, n_chunk=512),
        out_shape=jax.ShapeDtypeStruct((n, d_in), BF16),
        grid=(n // tm,),
        in_specs=[
            pl.BlockSpec((tm, d), lambda i: (i, 0)),
            pl.BlockSpec((1, d), lambda i: (0, 0)),
            pl.BlockSpec((1, d_in), lambda i: (0, 0)),
            pl.BlockSpec((d, d_in), lambda i: (0, 0), pipeline_mode=pl.Buffered(1)),
        ],
        out_specs=pl.BlockSpec((tm, d_in), lambda i: (i, 0)),
        compiler_params=_params("parallel"),
        name="in_proj",
    )(x, g, col_scale, w)


def _attn_kernel(q_ref, k_ref, v_ref, g_ref, o_ref, q2_ref, v0_ref, v1_ref, acc_ref, z_ref, a_ref,
                 *, seq, pairs):
    nq = seq // BLK
    row = lax.broadcasted_iota(jnp.int32, (BLK, BLK), 0)
    col = lax.broadcasted_iota(jnp.int32, (BLK, BLK), 1)
    upper = jnp.where(row > col, 1.0, 0.0).astype(BF16)
    upper2 = jnp.concatenate([upper, upper], axis=0)
    causal = jnp.concatenate([col < row, col < row], axis=0)
    mean_mat = _same_head_mean_matrix()
    lane = lax.broadcasted_iota(jnp.int32, (1, pairs * LANES), 1) % LANES
    m0 = jnp.where(lane < HEAD_DIM, 1.0, 0.0).astype(BF16)
    m1 = jnp.where(lane < HEAD_DIM, 0.0, 1.0).astype(BF16)

    v0_ref[...] = v_ref[...] * m0
    v1_ref[...] = v_ref[...] * m1
    for p in range(pairs):
        ls = slice(p * LANES, (p + 1) * LANES)
        q2_ref[p, :, :BLK, :] = (q_ref[:, ls] * m0[:, ls]).reshape(nq, BLK, LANES)
        q2_ref[p, :, BLK:, :] = (q_ref[:, ls] * m1[:, ls]).reshape(nq, BLK, LANES)

    lanes = [slice(p * LANES, (p + 1) * LANES) for p in range(pairs)]

    def scores(i, j, slot):
        ks = pl.multiple_of(j * BLK, BLK)
        for p in range(pairs):
            z_ref[slot, p] = lax.dot_general(q2_ref[p, i], k_ref[pl.ds(ks, BLK), lanes[p]],
                                             (((1,), (1,)), ((), ())),
                                             preferred_element_type=F32)

    def weighted_values(j):
        ks = pl.multiple_of(j * BLK, BLK)
        for p in range(pairs):
            acc_ref[p] += (
                jnp.dot(a_ref[p, :BLK], v0_ref[pl.ds(ks, BLK), lanes[p]], preferred_element_type=F32)
                + jnp.dot(a_ref[p, BLK:], v1_ref[pl.ds(ks, BLK), lanes[p]], preferred_element_type=F32))

    def weights(zs, carries, mask):
        log_betas, sps, splits = [], [], []
        for z in zs:
            e = jnp.exp2(jnp.abs(z) * (-LOG2E))
            sp = jnp.maximum(z, 0.0) + jnp.log(1.0 + e)
            log_betas.append(z - sp)
            if mask is not None:
                sp = jnp.where(mask, sp, 0.0)
            sps.append(sp)
            splits.append(jnp.concatenate(_split_bf16(sp), axis=1))
        new_carries = []
        for p in range(pairs):
            rowsum = jnp.sum(sps[p], axis=-1, keepdims=True)
            new_carries.append(rowsum if carries is None else carries[p] + rowsum)
        if carries is None:
            alive = jnp.int32(1)
        else:
            low = functools.reduce(jnp.minimum, new_carries)
            alive = (jnp.min(low) < -UNDERFLOW_LOG).astype(jnp.int32)
        betweens = [jnp.dot(s, upper2, preferred_element_type=F32) for s in splits]
        for p in range(pairs):
            between = betweens[p] if carries is None else betweens[p] + carries[p]
            a = jnp.exp(log_betas[p] - between)
            if mask is not None:
                a = jnp.where(mask, a, 0.0)
            a_ref[p] = a.astype(BF16)
        return alive, tuple(new_carries)

    def q_block(i, carry):
        acc_ref[...] = jnp.zeros_like(acc_ref)
        scores(i, i, 0)
        scores(i, jnp.maximum(i - 1, 0), 1)
        alive, carries = weights([z_ref[0, p] for p in range(pairs)], None, causal)

        def more_keys(state):
            return jnp.logical_and(state[0] < i, state[1] > 0)

        def key_block(state):
            jj, cs = state[0], state[2]
            j = i - 1 - jj
            slot = (jj + 1) % 2
            zs = [z_ref[slot, p] for p in range(pairs)]
            weighted_values(j + 1)
            scores(i, jnp.maximum(j - 1, 0), 1 - slot)
            alive, cs = weights(zs, cs, None)
            return jj + 1, alive, cs

        done = lax.while_loop(more_keys, key_block, (jnp.int32(0), alive, carries))[0]
        weighted_values(i - done)
        qs = pl.multiple_of(i * BLK, BLK)
        normed = _head_rms_norm_many([acc_ref[p] for p in range(pairs)], mean_mat,
                                     [g_ref[:, ls] for ls in lanes])
        for p in range(pairs):
            o_ref[pl.ds(qs, BLK), lanes[p]] = normed[p].astype(o_ref.dtype)
        return carry

    lax.fori_loop(0, nq, q_block, 0)


def _attention(proj, g_sb, batch, seq, d_sb):
    n = proj.shape[0]
    pairs = d_sb // LANES
    blk = lambda off: pl.BlockSpec((seq, d_sb), lambda b: (b, off))
    return pl.pallas_call(
        functools.partial(_attn_kernel, seq=seq, pairs=pairs),
        out_shape=jax.ShapeDtypeStruct((n, d_sb), BF16),
        grid=(batch,),
        in_specs=[blk(0), blk(1), blk(2), pl.BlockSpec((1, d_sb), lambda b: (0, 0))],
        out_specs=pl.BlockSpec((seq, d_sb), lambda b: (b, 0)),
        scratch_shapes=[pltpu.VMEM((pairs, seq // BLK, 2 * BLK, LANES), BF16),
                        pltpu.VMEM((seq, d_sb), BF16),
                        pltpu.VMEM((seq, d_sb), BF16),
                        pltpu.VMEM((pairs, BLK, LANES), F32),
                        pltpu.VMEM((2, pairs, 2 * BLK, LANES), F32),
                        pltpu.VMEM((pairs, 2 * BLK, LANES), BF16)],
        compiler_params=_params("parallel"),
        name="stickbreak_attn",
    )(proj, proj, proj, g_sb)


def _sgu_kernel(u_ref, gt_ref, w_ref, b_ref, gs_ref, go_ref, o_ref):
    tm, d_sgu = o_ref.shape
    row = lax.broadcasted_iota(jnp.int32, (BLK, BLK), 0)
    col = lax.broadcasted_iota(jnp.int32, (BLK, BLK), 1)
    tril = col <= row
    head0 = col < HEAD_DIM
    mean_mat = _same_head_mean_matrix()
    chunks = [slice(c * BLK, (c + 1) * BLK) for c in range(tm // BLK)]
    for p in range(d_sgu // LANES):
        ls = slice(p * LANES, (p + 1) * LANES)
        w01 = jnp.concatenate([jnp.where(tril, w_ref[2 * p], 0.0),
                               jnp.where(tril, w_ref[2 * p + 1], 0.0)], axis=0).astype(BF16)
        gates = [jax.nn.gelu(gt_ref[rs, ls].astype(F32)) for rs in chunks]
        gates = _head_rms_norm_many(gates, mean_mat, gs_ref[:, ls])
        mixes = [jnp.dot(w01, g.astype(BF16), preferred_element_type=F32) for g in gates]
        ys = [jax.nn.gelu(u_ref[rs, ls].astype(F32))
              * (jnp.where(head0, m[:BLK], m[BLK:]) + b_ref[:, ls])
              for rs, m in zip(chunks, mixes)]
        for rs, y in zip(chunks, _head_rms_norm_many(ys, mean_mat, go_ref[:, ls])):
            o_ref[rs, ls] = y.astype(o_ref.dtype)


def _sgu(proj, w_s, bias, g_gate, g_o, d_sb, d_sgu, tm):
    n = proj.shape[0]
    u_blk = 3 * d_sb // d_sgu
    return pl.pallas_call(
        _sgu_kernel,
        out_shape=jax.ShapeDtypeStruct((n, d_sgu), BF16),
        grid=(n // tm,),
        in_specs=[
            pl.BlockSpec((tm, d_sgu), lambda i: (i, u_blk)),
            pl.BlockSpec((tm, d_sgu), lambda i: (i, u_blk + 1)),
            pl.BlockSpec(w_s.shape, lambda i: (0, 0, 0)),
            pl.BlockSpec(bias.shape, lambda i: (0, 0)),
            pl.BlockSpec((1, d_sgu), lambda i: (0, 0)),
            pl.BlockSpec((1, d_sgu), lambda i: (0, 0)),
        ],
        out_specs=pl.BlockSpec((tm, d_sgu), lambda i: (i, 0)),
        compiler_params=_params("parallel"),
        name="spatial_gating",
    )(proj, proj, w_s, bias, g_gate, g_o)


def _top2_routes(logits_blocks, n_experts):
    lane = lax.broadcasted_iota(jnp.int32, logits_blocks[0].shape, 1).astype(F32)
    neg = jnp.float32(-jnp.inf)
    first_lane = lambda hit: jnp.min(jnp.where(hit, lane, float(LANES)), axis=-1, keepdims=True)
    lgs = [jnp.where(lane < n_experts, lg, neg) for lg in logits_blocks]
    m1s = [jnp.max(lg, axis=-1, keepdims=True) for lg in lgs]
    i1s = [first_lane(lg == m1) for lg, m1 in zip(lgs, m1s)]
    lg2s = [jnp.where(lane == i1, neg, lg) for lg, i1 in zip(lgs, i1s)]
    m2s = [jnp.max(lg2, axis=-1, keepdims=True) for lg2 in lg2s]
    i2s = [first_lane(lg2 == m2) for lg2, m2 in zip(lg2s, m2s)]
    routes = []
    for m1, m2, i1, i2 in zip(m1s, m2s, i1s, i2s):
        e2 = jnp.exp(m2 - m1)
        w1 = 1.0 / (1.0 + e2)
        w2 = e2 / (1.0 + e2)
        routes.append(jnp.where(lane == 0.0, i1, 0.0) + jnp.where(lane == 1.0, i2, 0.0)
                      + jnp.where(lane == 2.0, w1, 0.0) + jnp.where(lane == 3.0, w2, 0.0))
    return routes


def _out_proj_kernel(*refs, n_experts):
    if n_experts:
        ysb_ref, ysgu_ref, x_ref, w_ref, g_ref, rw_ref, xo_ref, h_ref, gates_ref = refs
    else:
        ysb_ref, ysgu_ref, x_ref, w_ref, g_ref, xo_ref, h_ref = refs
    y = jnp.concatenate([ysb_ref[...], ysgu_ref[...]], axis=1)
    xn = x_ref[...] + jnp.dot(y, w_ref[...], preferred_element_type=F32)
    xo_ref[...] = xn
    ms = jnp.mean(xn * xn, axis=-1, keepdims=True)
    h = xn * lax.rsqrt(ms + EPS) * g_ref[...]
    if n_experts:
        _store_token_tiles(h_ref, h)
    else:
        h_ref[...] = h.astype(h_ref.dtype)
    if n_experts:
        w_hi, w_lo = _split_bf16(rw_ref[...])
        w_both = jnp.concatenate([w_hi, w_lo], axis=1)
        blocks = [slice(r * BLK, (r + 1) * BLK) for r in range(h.shape[0] // BLK)]
        splits = [_split_bf16(h[rs]) for rs in blocks]
        firsts = [jnp.dot(hi, w_both, preferred_element_type=F32) for hi, _ in splits]
        seconds = [jnp.dot(lo, w_hi, preferred_element_type=F32) for _, lo in splits]
        routes = _top2_routes([a[:, :LANES] + a[:, LANES:] + b for a, b in zip(firsts, seconds)],
                              n_experts)
        for rs, route in zip(blocks, routes):
            gates_ref[rs, :] = route


def _out_proj(ysb, ysgu, x, w, g, router_w, n_experts, tm):
    n, d = x.shape
    d_half = ysb.shape[1]
    row_blk = lambda width: pl.BlockSpec((tm, width), lambda i: (i, 0))
    const = lambda shape: pl.BlockSpec(shape, lambda i: (0, 0))
    in_specs = [row_blk(d_half), row_blk(ysgu.shape[1]), row_blk(d), const(w.shape), const((1, d))]
    if n_experts:
        assert d == SUBLANES * LANES
        out_shape = [jax.ShapeDtypeStruct((n, d), F32),
                     jax.ShapeDtypeStruct((n * SUBLANES, LANES), F32)]
        out_specs = [row_blk(d), pl.BlockSpec((tm * SUBLANES, LANES), lambda i: (i, 0))]
    else:
        out_shape = [jax.ShapeDtypeStruct((n, d), F32), jax.ShapeDtypeStruct((n, d), BF16)]
        out_specs = [row_blk(d), row_blk(d)]
    args = [ysb, ysgu, x, w, g]
    if n_experts:
        in_specs.append(const(router_w.shape))
        out_shape.append(jax.ShapeDtypeStruct((n, LANES), F32))
        out_specs.append(row_blk(LANES))
        args.append(router_w)
    return pl.pallas_call(
        functools.partial(_out_proj_kernel, n_experts=n_experts),
        out_shape=out_shape,
        grid=(n // tm,),
        in_specs=in_specs,
        out_specs=out_specs,
        compiler_params=_params("parallel"),
        name="out_proj_router" if n_experts else "out_proj",
    )(*args)


def _final_norm(x, gain):
    ms = jnp.mean(x * x, axis=-1, keepdims=True)
    return x * lax.rsqrt(ms + EPS) * gain


def _ffn_kernel(x_ref, h_ref, wg_ref, wu_ref, wd_ref, gf_ref, o_ref, *, final):
    h = h_ref[...]
    acc = x_ref[...]
    for c in range(wg_ref.shape[0]):
        g = jnp.dot(h, wg_ref[c], preferred_element_type=F32)
        u = jnp.dot(h, wu_ref[c], preferred_element_type=F32)
        act = (jax.nn.silu(g) * u).astype(BF16)
        acc = acc + jnp.dot(act, wd_ref[c], preferred_element_type=F32)
    o_ref[...] = _final_norm(acc, gf_ref[...]) if final else acc


def _ffn(x, h, wg, wu, wd, g_final, final, tm):
    n, d = x.shape
    const3 = lambda a: pl.BlockSpec(a.shape, lambda i: (0, 0, 0), pipeline_mode=pl.Buffered(1))
    return pl.pallas_call(
        functools.partial(_ffn_kernel, final=final),
        out_shape=jax.ShapeDtypeStruct((n, d), F32),
        grid=(n // tm,),
        in_specs=[
            pl.BlockSpec((tm, d), lambda i: (i, 0)),
            pl.BlockSpec((tm, d), lambda i: (i, 0)),
            const3(wg), const3(wu), const3(wd),
            pl.BlockSpec((1, d), lambda i: (0, 0)),
        ],
        out_specs=pl.BlockSpec((tm, d), lambda i: (i, 0)),
        compiler_params=_params("parallel"),
        name="dense_swiglu",
    )(x, h, wg, wu, wd, g_final)


def _ff_chunks(d_ff):
    starts = list(range(0, d_ff, FF_CHUNK))
    return [(s, min(FF_CHUNK, d_ff - s)) for s in starts]


def _store_token_tiles(ref, value):
    rows = value.shape[0]
    for k in range(SUBLANES):
        ref[pl.ds(k, rows, stride=SUBLANES), :] = value[:, k * LANES:(k + 1) * LANES]


def _load_token_tiles(ref, first, rows):
    return jnp.concatenate([ref[pl.ds(first * SUBLANES + k, rows, stride=SUBLANES), :]
                            for k in range(SUBLANES)], axis=1)


def _dispatch_kernel(fill_ref, h_ref, idx_hbm, xs_hbm, idx_smem, zero_buf, row_sem, idx_sem,
                     fill_sem):
    i = pl.program_id(0)
    last = pl.num_programs(0) - 1
    slot = i % 2
    nxt = jnp.minimum(i + 1, last)
    tc = h_ref.shape[0] // SUBLANES

    def idx_copy(tile, s):
        return pltpu.make_async_copy(idx_hbm.at[tile], idx_smem.at[s], idx_sem.at[s])

    @pl.when(i == 0)
    def _():
        idx_copy(0, 0).start()
        zero_buf[...] = jnp.zeros_like(zero_buf)
        for j in range(fill_ref.shape[0]):
            first_row = pl.multiple_of(fill_ref[j] * SUBLANES, SUBLANES)
            fill = pltpu.make_async_copy(zero_buf, xs_hbm.at[pl.ds(first_row, zero_buf.shape[0])],
                                         fill_sem)
            fill.start()
            fill.wait()

    idx_copy(i, slot).wait()
    idx_copy(nxt, 1 - slot).start()
    for r in range(tc):
        for k in range(2):
            row = pl.multiple_of(idx_smem[slot, k * tc + r] * SUBLANES, SUBLANES)
            pltpu.make_async_copy(h_ref.at[pl.ds(r * SUBLANES, SUBLANES)],
                                  xs_hbm.at[pl.ds(row, SUBLANES)], row_sem).start(priority=k)
    for _ in range(2):
        pltpu.make_async_copy(h_ref, xs_hbm.at[pl.ds(0, tc * SUBLANES)], row_sem).wait()

    @pl.when(i == last)
    def _():
        idx_copy(nxt, 1 - slot).wait()


def _dispatch(h_tiles, dest, fill_start, n_rows, tm, tc):
    n_tok = h_tiles.shape[0] // SUBLANES
    grid_spec = pltpu.PrefetchScalarGridSpec(
        num_scalar_prefetch=1,
        grid=(n_tok // tc,),
        in_specs=[pl.BlockSpec((tc * SUBLANES, LANES), lambda i, fill: (i, 0)),
                  pl.BlockSpec(memory_space=pl.ANY)],
        out_specs=pl.BlockSpec(memory_space=pl.ANY),
        scratch_shapes=[pltpu.SMEM((2, 2 * tc), jnp.int32),
                        pltpu.VMEM((tm * SUBLANES, LANES), F32),
                        pltpu.SemaphoreType.DMA, pltpu.SemaphoreType.DMA((2,)),
                        pltpu.SemaphoreType.DMA],
    )
    return pl.pallas_call(
        _dispatch_kernel,
        out_shape=jax.ShapeDtypeStruct((n_rows * SUBLANES, LANES), F32),
        grid_spec=grid_spec,
        compiler_params=_params("arbitrary"),
        name="expert_dispatch",
    )(fill_start, h_tiles, dest)


def _grouped_ffn_kernel(te_ref, xs_ref, wg_ref, wu_ref, wd_ref, y_ref):
    del te_ref
    tm = xs_ref.shape[0] // SUBLANES
    x = _load_token_tiles(xs_ref, 0, tm).astype(BF16)
    acc = None
    for s, w in _ff_chunks(wg_ref.shape[2]):
        g = jnp.dot(x, wg_ref[0, :, s:s + w], preferred_element_type=F32)
        u = jnp.dot(x, wu_ref[0, :, s:s + w], preferred_element_type=F32)
        act = (jax.nn.silu(g) * u).astype(BF16)
        part = jnp.dot(act, wd_ref[0, s:s + w, :], preferred_element_type=F32)
        acc = part if acc is None else acc + part
    _store_token_tiles(y_ref, acc)


def _grouped_ffn(xs, tile_expert, wg, wu, wd, tm):
    n_tiles = tile_expert.shape[0]
    _, d, d_ff = wg.shape
    rows = pl.BlockSpec((tm * SUBLANES, LANES), lambda i, te: (i, 0))
    grid_spec = pltpu.PrefetchScalarGridSpec(
        num_scalar_prefetch=1,
        grid=(n_tiles,),
        in_specs=[
            rows,
            pl.BlockSpec((1, d, d_ff), lambda i, te: (te[i], 0, 0)),
            pl.BlockSpec((1, d, d_ff), lambda i, te: (te[i], 0, 0)),
            pl.BlockSpec((1, d_ff, d), lambda i, te: (te[i], 0, 0)),
        ],
        out_specs=rows,
    )
    return pl.pallas_call(
        _grouped_ffn_kernel,
        out_shape=jax.ShapeDtypeStruct(xs.shape, F32),
        grid_spec=grid_spec,
        compiler_params=_params("arbitrary"),
        name="expert_swiglu",
    )(tile_expert, xs, wg, wu, wd)


def _combine_kernel(x_ref, route_ref, gf_ref, y_hbm, idx_hbm, o_ref,
                    ybuf, idx_smem, row_sem, idx_sem, *, final):
    tc = o_ref.shape[0]
    n_rows = 2 * tc
    i = pl.program_id(0)
    last = pl.num_programs(0) - 1
    slot = i % 2
    nxt = jnp.minimum(i + 1, last)
    nxt2 = jnp.minimum(i + 2, last)

    def idx_copy(tile, s):
        return pltpu.make_async_copy(idx_hbm.at[tile], idx_smem.at[s], idx_sem.at[s])

    def start_rows(s):
        for r in range(n_rows):
            row = pl.multiple_of(idx_smem[s, r] * SUBLANES, SUBLANES)
            pltpu.make_async_copy(y_hbm.at[pl.ds(row, SUBLANES)],
                                  ybuf.at[s, pl.ds(r * SUBLANES, SUBLANES)],
                                  row_sem.at[s]).start(priority=r % 2)

    def wait_rows(s):
        pltpu.make_async_copy(y_hbm.at[pl.ds(0, n_rows * SUBLANES)], ybuf.at[s],
                              row_sem.at[s]).wait()

    @pl.when(i == 0)
    def _():
        first = idx_copy(0, 0)
        first.start()
        first.wait()
        start_rows(0)
        idx_copy(nxt, 1).start()

    wait_rows(slot)
    idx_copy(nxt, 1 - slot).wait()
    start_rows(1 - slot)
    idx_copy(nxt2, slot).start()

    route = route_ref[...]
    lane = lax.broadcasted_iota(jnp.int32, route.shape, 1)
    w1 = jnp.sum(jnp.where(lane == 2, route, 0.0), axis=-1, keepdims=True)
    w2 = jnp.sum(jnp.where(lane == 3, route, 0.0), axis=-1, keepdims=True)
    y_ref = ybuf.at[slot]
    out = x_ref[...] + w1 * _load_token_tiles(y_ref, 0, tc) + w2 * _load_token_tiles(y_ref, tc, tc)
    o_ref[...] = _final_norm(out, gf_ref[...]) if final else out

    @pl.when(i == last)
    def _():
        wait_rows(1 - slot)
        idx_copy(nxt2, slot).wait()


def _combine(x, route, g_final, y_sorted, dest, final, tc):
    n, d = x.shape
    row = lambda width: pl.BlockSpec((tc, width), lambda i: (i, 0))
    return pl.pallas_call(
        functools.partial(_combine_kernel, final=final),
        out_shape=jax.ShapeDtypeStruct((n, d), F32),
        grid=(n // tc,),
        in_specs=[row(d), row(LANES), pl.BlockSpec((1, d), lambda i: (0, 0)),
                  pl.BlockSpec(memory_space=pl.ANY), pl.BlockSpec(memory_space=pl.ANY)],
        out_specs=row(d),
        scratch_shapes=[pltpu.VMEM((2, 2 * tc * SUBLANES, LANES), F32),
                        pltpu.SMEM((2, 2 * tc), jnp.int32),
                        pltpu.SemaphoreType.DMA((2,)), pltpu.SemaphoreType.DMA((2,))],
        compiler_params=_params("arbitrary"),
        name="expert_combine",
    )(x, route, g_final, y_sorted, dest)


def _routing_tables(route, n_experts, tm, tc):
    n = route.shape[0]
    ids = route[:, :2].astype(jnp.int32)
    onehot = (ids[:, :, None] == jnp.arange(n_experts, dtype=jnp.int32)).any(axis=1).astype(jnp.int32)
    counts = onehot.sum(axis=0)
    group = (counts + tm - 1) // tm * tm
    group_end = jnp.cumsum(group)
    before = jnp.cumsum(onehot, axis=0) - onehot
    dest = (group_end - group)[ids] + jnp.take_along_axis(before, ids, axis=1)
    n_tiles = (2 * n) // tm + n_experts
    tile_start = jnp.arange(n_tiles, dtype=jnp.int32) * tm
    tile_expert = jnp.minimum(jnp.searchsorted(group_end, tile_start, side="right"),
                              n_experts - 1).astype(jnp.int32)
    dest_tiles = dest.reshape(n // tc, tc, 2).transpose(0, 2, 1).reshape(n // tc, 2 * tc)
    n_rows = n_tiles * tm
    pad_windows = jnp.minimum(group_end - group + counts, n_rows - tm)
    tail_windows = n_rows - tm * jnp.arange(1, n_experts + 1, dtype=jnp.int32)
    fill_start = jnp.concatenate([pad_windows, tail_windows]).astype(jnp.int32)
    return n_rows, tile_expert, dest_tiles, fill_start


def _pick_tile(n, want):
    tm = min(n, want)
    assert n % tm == 0 and tm % BLK == 0, (n, tm)
    return tm


def kernel(x, w_in, w_out, g_mix, g_ffn, g_sgu, sgu_w, sgu_b, g_out, ffn_w_gate, ffn_w_up,
           ffn_w_down, router_w, moe_w_gate, moe_w_up, moe_w_down, g_final):
    batch, seq, d = x.shape
    n = batch * seq
    depth = w_in.shape[0]
    n_groups = sgu_w.shape[1]
    d_sgu = n_groups * HEAD_DIM
    d_in = w_in.shape[2]
    d_sb = (d_in - 2 * d_sgu) // 3
    n_experts = router_w.shape[2]
    assert seq % BLK == 0 and sgu_w.shape[2] == BLK and d_sb % LANES == 0 and d_sgu % LANES == 0
    assert 3 * d_sb % d_sgu == 0 and d % LANES == 0

    col_scale = jnp.where(jnp.arange(d_in) < d_sb, HEAD_DIM ** -0.5, 1.0).astype(F32)[None, :]
    row2 = lambda a: a.reshape(1, -1).astype(F32)
    g_fin = row2(g_final)

    xf = x.reshape(n, d)
    for l in range(depth):
        last = l == depth - 1
        proj = _in_proj(xf, row2(g_mix[l]), col_scale, w_in[l].astype(BF16), _pick_tile(n, 512))
        y_sb = _attention(proj, row2(g_out[l, :d_sb]), batch, seq, d_sb)
        bias = jnp.repeat(sgu_b[l].T, HEAD_DIM, axis=1)
        y_sgu = _sgu(proj, sgu_w[l], bias, row2(g_sgu[l]), row2(g_out[l, d_sb:]), d_sb, d_sgu,
                     _pick_tile(n, 512))
        i = l // 2
        if l % 2 == 0:
            xf, h2 = _out_proj(y_sb, y_sgu, xf, w_out[l].astype(BF16), row2(g_ffn[l]), None, 0,
                               _pick_tile(n, 512))
            d_ff = ffn_w_gate.shape[2]
            assert d_ff % FF_CHUNK == 0
            nc = d_ff // FF_CHUNK
            wg = ffn_w_gate[i].astype(BF16).reshape(d, nc, FF_CHUNK).transpose(1, 0, 2)
            wu = ffn_w_up[i].astype(BF16).reshape(d, nc, FF_CHUNK).transpose(1, 0, 2)
            wd = ffn_w_down[i].astype(BF16).reshape(nc, FF_CHUNK, d)
            xf = _ffn(xf, h2, wg, wu, wd, g_fin, last, _pick_tile(n, 512))
        else:
            rw = jnp.pad(router_w[i].astype(F32), ((0, 0), (0, LANES - n_experts)))
            xf, h2, route = _out_proj(y_sb, y_sgu, xf, w_out[l].astype(BF16), row2(g_ffn[l]), rw,
                                      n_experts, _pick_tile(n, 512))
            tm, tc = _pick_tile(2 * n, MOE_ROW_TILE), _pick_tile(n, MOE_TOKEN_TILE)
            n_rows, tile_expert, dest, fill_start = _routing_tables(route, n_experts, tm, tc)
            x_sorted = _dispatch(h2, dest, fill_start, n_rows, tm, tc)
            y_sorted = _grouped_ffn(x_sorted, tile_expert, moe_w_gate[i].astype(BF16),
                                    moe_w_up[i].astype(BF16), moe_w_down[i].astype(BF16), tm)
            xf = _combine(xf, route, g_fin, y_sorted, dest, last, tc)
    return xf.reshape(batch, seq, d)
```

```python
import functools

import jax
import jax.numpy as jnp
from jax import lax
from jax.experimental import pallas as pl
from jax.experimental.pallas import tpu as pltpu

HEAD_DIM = 64
BLK = 128
LANES = 128
SUBLANES = 8
EPS = 1e-6
LOG2E = 1.4426950408889634
UNDERFLOW_LOG = -104.0
FF_CHUNK = 256
MOE_ROW_TILE = 512
MOE_TOKEN_TILE = 512
VMEM_LIMIT = 56 * 1024 * 1024

F32 = jnp.float32
BF16 = jnp.bfloat16


def _params(*sem):
    return pltpu.CompilerParams(dimension_semantics=sem, vmem_limit_bytes=VMEM_LIMIT)


def _split_bf16(a):
    hi = a.astype(BF16)
    lo = (a - hi.astype(F32)).astype(BF16)
    return hi, lo


def _same_head_mean_matrix():
    r = lax.broadcasted_iota(jnp.int32, (LANES, LANES), 0)
    c = lax.broadcasted_iota(jnp.int32, (LANES, LANES), 1)
    return jnp.where((r < HEAD_DIM) == (c < HEAD_DIM), 1.0 / HEAD_DIM, 0.0).astype(BF16)


def _head_rms_norm_many(ys, mean_mat, gain):
    mean2 = jnp.concatenate([mean_mat, mean_mat], axis=0)
    splits = [jnp.concatenate(_split_bf16(y * y), axis=1) for y in ys]
    means = [jnp.dot(s, mean2, preferred_element_type=F32) for s in splits]
    gains = gain if isinstance(gain, (list, tuple)) else [gain] * len(ys)
    return [y * lax.rsqrt(ms + EPS) * g for y, ms, g in zip(ys, means, gains)]


def _head_rms_norm(y, mean_mat, gain):
    return _head_rms_norm_many([y], mean_mat, gain)[0]


def _in_proj_kernel(x_ref, g_ref, cs_ref, w_ref, o_ref, *, n_chunk):
    x = x_ref[...]
    ms = jnp.mean(x * x, axis=-1, keepdims=True)
    h = (x * lax.rsqrt(ms + EPS) * g_ref[...]).astype(BF16)
    for c in range(o_ref.shape[1] // n_chunk):
        sl = slice(c * n_chunk, (c + 1) * n_chunk)
        acc = jnp.dot(h, w_ref[:, sl], preferred_element_type=F32)
        o_ref[:, sl] = (acc * cs_ref[:, sl]).astype(o_ref.dtype)


def _in_proj(x, g, col_scale, w, tm):
    n, d = x.shape
    d_in = w.shape[1]
    return pl.pallas_call(
        functools.partial(_in_proj_kernel, n_chunk=512),
        out_shape=jax.ShapeDtypeStruct((n, d_in), BF16),
        grid=(n // tm,),
        in_specs=[
            pl.BlockSpec((tm, d), lambda i: (i, 0)),
            pl.BlockSpec((1, d), lambda i: (0, 0)),
            pl.BlockSpec((1, d_in), lambda i: (0, 0)),
            pl.BlockSpec((d, d_in), lambda i: (0, 0), pipeline_mode=pl.Buffered(1)),
        ],
        out_specs=pl.BlockSpec((tm, d_in), lambda i: (i, 0)),
        compiler_params=_params("parallel"),
        name="in_proj",
    )(x, g, col_scale, w)


def _attn_kernel(q_ref, k_ref, v_ref, g_ref, o_ref, q2_ref, v0_ref, v1_ref, acc_ref, z_ref, a_ref,
                 *, seq, pairs):
    nq = seq // BLK
    row = lax.broadcasted_iota(jnp.int32, (BLK, BLK), 0)
    col = lax.broadcasted_iota(jnp.int32, (BLK, BLK), 1)
    upper = jnp.where(row > col, 1.0, 0.0).astype(BF16)
    upper2 = jnp.concatenate([upper, upper], axis=0)
    causal = jnp.concatenate([col < row, col < row], axis=0)
    mean_mat = _same_head_mean_matrix()
    lane = lax.broadcasted_iota(jnp.int32, (1, pairs * LANES), 1) % LANES
    m0 = jnp.where(lane < HEAD_DIM, 1.0, 0.0).astype(BF16)
    m1 = jnp.where(lane < HEAD_DIM, 0.0, 1.0).astype(BF16)

    v0_ref[...] = v_ref[...] * m0
    v1_ref[...] = v_ref[...] * m1
    for p in range(pairs):
        ls = slice(p * LANES, (p + 1) * LANES)
        q2_ref[p, :, :BLK, :] = (q_ref[:, ls] * m0[:, ls]).reshape(nq, BLK, LANES)
        q2_ref[p, :, BLK:, :] = (q_ref[:, ls] * m1[:, ls]).reshape(nq, BLK, LANES)

    lanes = [slice(p * LANES, (p + 1) * LANES) for p in range(pairs)]

    def scores(i, j, slot):
        ks = pl.multiple_of(j * BLK, BLK)
        for p in range(pairs):
            z_ref[slot, p] = lax.dot_general(q2_ref[p, i], k_ref[pl.ds(ks, BLK), lanes[p]],
                                             (((1,), (1,)), ((), ())),
                                             preferred_element_type=F32)

    def weighted_values(par, j):
        ks = pl.multiple_of(j * BLK, BLK)
        for p in range(pairs):
            acc_ref[par, p] += (
                jnp.dot(a_ref[par, p, :BLK], v0_ref[pl.ds(ks, BLK), lanes[p]],
                        preferred_element_type=F32)
                + jnp.dot(a_ref[par, p, BLK:], v1_ref[pl.ds(ks, BLK), lanes[p]],
                          preferred_element_type=F32))

    def weights(par, zs, carries, mask):
        log_betas, sps, splits = [], [], []
        for z in zs:
            e = jnp.exp2(jnp.abs(z) * (-LOG2E))
            sp = jnp.maximum(z, 0.0) + jnp.log(1.0 + e)
            log_betas.append(z - sp)
            if mask is not None:
                sp = jnp.where(mask, sp, 0.0)
            sps.append(sp)
            splits.append(jnp.concatenate(_split_bf16(sp), axis=1))
        new_carries = []
        for p in range(pairs):
            rowsum = jnp.sum(sps[p], axis=-1, keepdims=True)
            new_carries.append(rowsum if carries is None else carries[p] + rowsum)
        if carries is None:
            alive = jnp.int32(1)
        else:
            low = functools.reduce(jnp.minimum, new_carries)
            alive = (jnp.min(low) < -UNDERFLOW_LOG).astype(jnp.int32)
        betweens = [jnp.dot(s, upper2, preferred_element_type=F32) for s in splits]
        for p in range(pairs):
            between = betweens[p] if carries is None else betweens[p] + carries[p]
            a = jnp.exp(log_betas[p] - between)
            if mask is not None:
                a = jnp.where(mask, a, 0.0)
            a_ref[par, p] = a.astype(BF16)
        return alive, tuple(new_carries)

    def diagonal(i, par):
        acc_ref[par] = jnp.zeros(acc_ref.shape[1:], F32)
        scores(i, i, 0)
        scores(i, jnp.maximum(i - 1, 0), 1)

    def finish(i, par, last_block):
        weighted_values(par, last_block)
        qs = pl.multiple_of(i * BLK, BLK)
        normed = _head_rms_norm_many([acc_ref[par, p] for p in range(pairs)], mean_mat,
                                     [g_ref[:, ls] for ls in lanes])
        for p in range(pairs):
            o_ref[pl.ds(qs, BLK), lanes[p]] = normed[p].astype(o_ref.dtype)

    def q_block(i, pending):
        par = i % 2
        diagonal(i, par)
        finish(i - 1, 1 - par, pending)
        alive, carries = weights(par, [z_ref[0, p] for p in range(pairs)], None, causal)

        def more_keys(state):
            return jnp.logical_and(state[0] < i, state[1] > 0)

        def key_block(state):
            jj, cs = state[0], state[2]
            j = i - 1 - jj
            slot = (jj + 1) % 2
            zs = [z_ref[slot, p] for p in range(pairs)]
            weighted_values(par, j + 1)
            scores(i, jnp.maximum(j - 1, 0), 1 - slot)
            alive, cs = weights(par, zs, cs, None)
            return jj + 1, alive, cs

        done = lax.while_loop(more_keys, key_block, (jnp.int32(0), alive, carries))[0]
        return i - done

    diagonal(0, 0)
    weights(0, [z_ref[0, p] for p in range(pairs)], None, causal)
    pending = lax.fori_loop(1, nq, q_block, jnp.int32(0))
    finish(nq - 1, (nq - 1) % 2, pending)


def _attention(proj, g_sb, batch, seq, d_sb):
    n = proj.shape[0]
    pairs = d_sb // LANES
    blk = lambda off: pl.BlockSpec((seq, d_sb), lambda b: (b, off))
    return pl.pallas_call(
        functools.partial(_attn_kernel, seq=seq, pairs=pairs),
        out_shape=jax.ShapeDtypeStruct((n, d_sb), BF16),
        grid=(batch,),
        in_specs=[blk(0), blk(1), blk(2), pl.BlockSpec((1, d_sb), lambda b: (0, 0))],
        out_specs=pl.BlockSpec((seq, d_sb), lambda b: (b, 0)),
        scratch_shapes=[pltpu.VMEM((pairs, seq // BLK, 2 * BLK, LANES), BF16),
                        pltpu.VMEM((seq, d_sb), BF16),
                        pltpu.VMEM((seq, d_sb), BF16),
                        pltpu.VMEM((2, pairs, BLK, LANES), F32),
                        pltpu.VMEM((2, pairs, 2 * BLK, LANES), F32),
                        pltpu.VMEM((2, pairs, 2 * BLK, LANES), BF16)],
        compiler_params=_params("parallel"),
        name="stickbreak_attn",
    )(proj, proj, proj, g_sb)


def _sgu_kernel(u_ref, gt_ref, w_ref, b_ref, gs_ref, go_ref, o_ref):
    tm, d_sgu = o_ref.shape
    row = lax.broadcasted_iota(jnp.int32, (BLK, BLK), 0)
    col = lax.broadcasted_iota(jnp.int32, (BLK, BLK), 1)
    tril = col <= row
    head0 = col < HEAD_DIM
    mean_mat = _same_head_mean_matrix()
    chunks = [slice(c * BLK, (c + 1) * BLK) for c in range(tm // BLK)]
    for p in range(d_sgu // LANES):
        ls = slice(p * LANES, (p + 1) * LANES)
        w01 = jnp.concatenate([jnp.where(tril, w_ref[2 * p], 0.0),
                               jnp.where(tril, w_ref[2 * p + 1], 0.0)], axis=0).astype(BF16)
        gates = [jax.nn.gelu(gt_ref[rs, ls].astype(F32)) for rs in chunks]
        gates = _head_rms_norm_many(gates, mean_mat, gs_ref[:, ls])
        mixes = [jnp.dot(w01, g.astype(BF16), preferred_element_type=F32) for g in gates]
        ys = [jax.nn.gelu(u_ref[rs, ls].astype(F32))
              * (jnp.where(head0, m[:BLK], m[BLK:]) + b_ref[:, ls])
              for rs, m in zip(chunks, mixes)]
        for rs, y in zip(chunks, _head_rms_norm_many(ys, mean_mat, go_ref[:, ls])):
            o_ref[rs, ls] = y.astype(o_ref.dtype)


def _sgu(proj, w_s, bias, g_gate, g_o, d_sb, d_sgu, tm):
    n = proj.shape[0]
    u_blk = 3 * d_sb // d_sgu
    return pl.pallas_call(
        _sgu_kernel,
        out_shape=jax.ShapeDtypeStruct((n, d_sgu), BF16),
        grid=(n // tm,),
        in_specs=[
            pl.BlockSpec((tm, d_sgu), lambda i: (i, u_blk)),
            pl.BlockSpec((tm, d_sgu), lambda i: (i, u_blk + 1)),
            pl.BlockSpec(w_s.shape, lambda i: (0, 0, 0)),
            pl.BlockSpec(bias.shape, lambda i: (0, 0)),
            pl.BlockSpec((1, d_sgu), lambda i: (0, 0)),
            pl.BlockSpec((1, d_sgu), lambda i: (0, 0)),
        ],
        out_specs=pl.BlockSpec((tm, d_sgu), lambda i: (i, 0)),
        compiler_params=_params("parallel"),
        name="spatial_gating",
    )(proj, proj, w_s, bias, g_gate, g_o)


def _top2_routes(logits_blocks, n_experts):
    shape = logits_blocks[0].shape
    assert shape == (BLK, LANES)
    lane = lax.broadcasted_iota(jnp.int32, shape, 1).astype(F32)
    neg = jnp.float32(-jnp.inf)
    first_lane = lambda hit: jnp.min(jnp.where(hit, lane, float(LANES)), axis=-1, keepdims=True)
    lgs = [jnp.where(lane < n_experts, lg, neg) for lg in logits_blocks]
    m1s = [jnp.max(lg, axis=-1, keepdims=True) for lg in lgs]
    i1s = [first_lane(lg == m1) for lg, m1 in zip(lgs, m1s)]
    lg2s = [jnp.where(lane == i1, neg, lg) for lg, i1 in zip(lgs, i1s)]
    m2s = [jnp.max(lg2, axis=-1, keepdims=True) for lg2 in lg2s]
    i2s = [first_lane(lg2 == m2) for lg2, m2 in zip(lg2s, m2s)]
    r = lax.broadcasted_iota(jnp.int32, shape, 0)
    c = lax.broadcasted_iota(jnp.int32, shape, 1)
    earlier = jnp.where(c < r, 1.0, 0.0).astype(BF16)
    ones = jnp.ones(shape, BF16)
    chosen = [jnp.where((lane == i1) | (lane == i2), 1.0, 0.0).astype(BF16)
              for i1, i2 in zip(i1s, i2s)]
    within = [jnp.dot(earlier, m, preferred_element_type=F32) for m in chosen]
    totals = [jnp.dot(ones, m, preferred_element_type=F32) for m in chosen]
    routes = []
    before = None
    for m1, m2, i1, i2, rank in zip(m1s, m2s, i1s, i2s, within):
        if before is not None:
            rank = rank + before
        e2 = jnp.exp(m2 - m1)
        w1 = 1.0 / (1.0 + e2)
        w2 = e2 / (1.0 + e2)
        r1 = jnp.sum(jnp.where(lane == i1, rank, 0.0), axis=-1, keepdims=True)
        r2 = jnp.sum(jnp.where(lane == i2, rank, 0.0), axis=-1, keepdims=True)
        routes.append(jnp.where(lane == 0.0, i1, 0.0) + jnp.where(lane == 1.0, i2, 0.0)
                      + jnp.where(lane == 2.0, w1, 0.0) + jnp.where(lane == 3.0, w2, 0.0)
                      + jnp.where(lane == 4.0, r1, 0.0) + jnp.where(lane == 5.0, r2, 0.0))
        total = totals[len(routes) - 1]
        before = total if before is None else before + total
    return routes, before


def _out_proj_kernel(*refs, n_experts):
    if n_experts:
        ysb_ref, ysgu_ref, x_ref, w_ref, g_ref, rw_ref, xo_ref, h_ref, gates_ref, cnt_ref = refs
    else:
        ysb_ref, ysgu_ref, x_ref, w_ref, g_ref, xo_ref, h_ref = refs
    y = jnp.concatenate([ysb_ref[...], ysgu_ref[...]], axis=1)
    xn = x_ref[...] + jnp.dot(y, w_ref[...], preferred_element_type=F32)
    xo_ref[...] = xn
    ms = jnp.mean(xn * xn, axis=-1, keepdims=True)
    h = xn * lax.rsqrt(ms + EPS) * g_ref[...]
    if n_experts:
        _store_token_tiles(h_ref, h)
    else:
        h_ref[...] = h.astype(h_ref.dtype)
    if n_experts:
        w_hi, w_lo = _split_bf16(rw_ref[...])
        w_both = jnp.concatenate([w_hi, w_lo], axis=1)
        blocks = [slice(r * BLK, (r + 1) * BLK) for r in range(h.shape[0] // BLK)]
        splits = [_split_bf16(h[rs]) for rs in blocks]
        firsts = [jnp.dot(hi, w_both, preferred_element_type=F32) for hi, _ in splits]
        seconds = [jnp.dot(lo, w_hi, preferred_element_type=F32) for _, lo in splits]
        routes, counts = _top2_routes(
            [a[:, :LANES] + a[:, LANES:] + b for a, b in zip(firsts, seconds)], n_experts)
        for rs, route in zip(blocks, routes):
            gates_ref[rs, :] = route
        cnt_ref[...] = counts[:SUBLANES, :]


def _out_proj(ysb, ysgu, x, w, g, router_w, n_experts, tm):
    n, d = x.shape
    d_half = ysb.shape[1]
    row_blk = lambda width: pl.BlockSpec((tm, width), lambda i: (i, 0))
    const = lambda shape: pl.BlockSpec(shape, lambda i: (0, 0))
    in_specs = [row_blk(d_half), row_blk(ysgu.shape[1]), row_blk(d), const(w.shape), const((1, d))]
    if n_experts:
        assert d == SUBLANES * LANES
        out_shape = [jax.ShapeDtypeStruct((n, d), F32),
                     jax.ShapeDtypeStruct((n * SUBLANES, LANES), F32)]
        out_specs = [row_blk(d), pl.BlockSpec((tm * SUBLANES, LANES), lambda i: (i, 0))]
    else:
        out_shape = [jax.ShapeDtypeStruct((n, d), F32), jax.ShapeDtypeStruct((n, d), BF16)]
        out_specs = [row_blk(d), row_blk(d)]
    args = [ysb, ysgu, x, w, g]
    if n_experts:
        in_specs.append(const(router_w.shape))
        out_shape += [jax.ShapeDtypeStruct((n, LANES), F32),
                      jax.ShapeDtypeStruct((n // tm * SUBLANES, LANES), F32)]
        out_specs += [row_blk(LANES), pl.BlockSpec((SUBLANES, LANES), lambda i: (i, 0))]
        args.append(router_w)
    return pl.pallas_call(
        functools.partial(_out_proj_kernel, n_experts=n_experts),
        out_shape=out_shape,
        grid=(n // tm,),
        in_specs=in_specs,
        out_specs=out_specs,
        compiler_params=_params("parallel"),
        name="out_proj_router" if n_experts else "out_proj",
    )(*args)


def _final_norm(x, gain):
    ms = jnp.mean(x * x, axis=-1, keepdims=True)
    return x * lax.rsqrt(ms + EPS) * gain


def _ffn_kernel(x_ref, h_ref, wg_ref, wu_ref, wd_ref, gf_ref, o_ref, *, final):
    h = h_ref[...]
    acc = x_ref[...]
    for c in range(wg_ref.shape[0]):
        g = jnp.dot(h, wg_ref[c], preferred_element_type=F32)
        u = jnp.dot(h, wu_ref[c], preferred_element_type=F32)
        act = (jax.nn.silu(g) * u).astype(BF16)
        acc = acc + jnp.dot(act, wd_ref[c], preferred_element_type=F32)
    o_ref[...] = _final_norm(acc, gf_ref[...]) if final else acc


def _ffn(x, h, wg, wu, wd, g_final, final, tm):
    n, d = x.shape
    const3 = lambda a: pl.BlockSpec(a.shape, lambda i: (0, 0, 0), pipeline_mode=pl.Buffered(1))
    return pl.pallas_call(
        functools.partial(_ffn_kernel, final=final),
        out_shape=jax.ShapeDtypeStruct((n, d), F32),
        grid=(n // tm,),
        in_specs=[
            pl.BlockSpec((tm, d), lambda i: (i, 0)),
            pl.BlockSpec((tm, d), lambda i: (i, 0)),
            const3(wg), const3(wu), const3(wd),
            pl.BlockSpec((1, d), lambda i: (0, 0)),
        ],
        out_specs=pl.BlockSpec((tm, d), lambda i: (i, 0)),
        compiler_params=_params("parallel"),
        name="dense_swiglu",
    )(x, h, wg, wu, wd, g_final)


def _ff_chunks(d_ff):
    starts = list(range(0, d_ff, FF_CHUNK))
    return [(s, min(FF_CHUNK, d_ff - s)) for s in starts]


def _store_token_tiles(ref, value):
    rows = value.shape[0]
    for k in range(SUBLANES):
        ref[pl.ds(k, rows, stride=SUBLANES), :] = value[:, k * LANES:(k + 1) * LANES]


def _load_token_tiles(ref, first, rows):
    return jnp.concatenate([ref[pl.ds(first * SUBLANES + k, rows, stride=SUBLANES), :]
                            for k in range(SUBLANES)], axis=1)


def _dispatch_kernel(fill_ref, h_ref, idx_hbm, xs_hbm, idx_smem, zero_buf, row_sem, idx_sem,
                     fill_sem):
    i = pl.program_id(0)
    last = pl.num_programs(0) - 1
    slot = i % 2
    nxt = jnp.minimum(i + 1, last)
    tc = h_ref.shape[0] // SUBLANES

    def idx_copy(tile, s):
        return pltpu.make_async_copy(idx_hbm.at[tile], idx_smem.at[s], idx_sem.at[s])

    @pl.when(i == 0)
    def _():
        idx_copy(0, 0).start()
        zero_buf[...] = jnp.zeros_like(zero_buf)
        for j in range(fill_ref.shape[0]):
            first_row = pl.multiple_of(fill_ref[j] * SUBLANES, SUBLANES)
            fill = pltpu.make_async_copy(zero_buf, xs_hbm.at[pl.ds(first_row, zero_buf.shape[0])],
                                         fill_sem)
            fill.start()
            fill.wait()

    idx_copy(i, slot).wait()
    idx_copy(nxt, 1 - slot).start()
    for r in range(tc):
        for k in range(2):
            row = pl.multiple_of(idx_smem[slot, k * tc + r] * SUBLANES, SUBLANES)
            pltpu.make_async_copy(h_ref.at[pl.ds(r * SUBLANES, SUBLANES)],
                                  xs_hbm.at[pl.ds(row, SUBLANES)], row_sem).start(priority=k)
    for _ in range(2):
        pltpu.make_async_copy(h_ref, xs_hbm.at[pl.ds(0, tc * SUBLANES)], row_sem).wait()

    @pl.when(i == last)
    def _():
        idx_copy(nxt, 1 - slot).wait()


def _dispatch(h_tiles, dest, fill_start, n_rows, tm, tc):
    n_tok = h_tiles.shape[0] // SUBLANES
    grid_spec = pltpu.PrefetchScalarGridSpec(
        num_scalar_prefetch=1,
        grid=(n_tok // tc,),
        in_specs=[pl.BlockSpec((tc * SUBLANES, LANES), lambda i, fill: (i, 0)),
                  pl.BlockSpec(memory_space=pl.ANY)],
        out_specs=pl.BlockSpec(memory_space=pl.ANY),
        scratch_shapes=[pltpu.SMEM((2, 2 * tc), jnp.int32),
                        pltpu.VMEM((tm * SUBLANES, LANES), F32),
                        pltpu.SemaphoreType.DMA, pltpu.SemaphoreType.DMA((2,)),
                        pltpu.SemaphoreType.DMA],
    )
    return pl.pallas_call(
        _dispatch_kernel,
        out_shape=jax.ShapeDtypeStruct((n_rows * SUBLANES, LANES), F32),
        grid_spec=grid_spec,
        compiler_params=_params("arbitrary"),
        name="expert_dispatch",
    )(fill_start, h_tiles, dest)


def _grouped_ffn_kernel(te_ref, xs_ref, wg_ref, wu_ref, wd_ref, y_ref):
    del te_ref
    tm = xs_ref.shape[0] // SUBLANES
    x = _load_token_tiles(xs_ref, 0, tm).astype(BF16)
    acc = None
    for s, w in _ff_chunks(wg_ref.shape[2]):
        g = jnp.dot(x, wg_ref[0, :, s:s + w], preferred_element_type=F32)
        u = jnp.dot(x, wu_ref[0, :, s:s + w], preferred_element_type=F32)
        act = (jax.nn.silu(g) * u).astype(BF16)
        part = jnp.dot(act, wd_ref[0, s:s + w, :], preferred_element_type=F32)
        acc = part if acc is None else acc + part
    _store_token_tiles(y_ref, acc)


def _grouped_ffn(xs, tile_expert, wg, wu, wd, tm):
    n_tiles = tile_expert.shape[0]
    _, d, d_ff = wg.shape
    rows = pl.BlockSpec((tm * SUBLANES, LANES), lambda i, te: (i, 0))
    grid_spec = pltpu.PrefetchScalarGridSpec(
        num_scalar_prefetch=1,
        grid=(n_tiles,),
        in_specs=[
            rows,
            pl.BlockSpec((1, d, d_ff), lambda i, te: (te[i], 0, 0)),
            pl.BlockSpec((1, d, d_ff), lambda i, te: (te[i], 0, 0)),
            pl.BlockSpec((1, d_ff, d), lambda i, te: (te[i], 0, 0)),
        ],
        out_specs=rows,
    )
    return pl.pallas_call(
        _grouped_ffn_kernel,
        out_shape=jax.ShapeDtypeStruct(xs.shape, F32),
        grid_spec=grid_spec,
        compiler_params=_params("arbitrary"),
        name="expert_swiglu",
    )(tile_expert, xs, wg, wu, wd)


def _combine_kernel(x_ref, route_ref, gf_ref, y_hbm, idx_hbm, o_ref,
                    ybuf, idx_smem, row_sem, idx_sem, *, final):
    tc = o_ref.shape[0]
    n_rows = 2 * tc
    i = pl.program_id(0)
    last = pl.num_programs(0) - 1
    slot = i % 2
    nxt = jnp.minimum(i + 1, last)
    nxt2 = jnp.minimum(i + 2, last)

    def idx_copy(tile, s):
        return pltpu.make_async_copy(idx_hbm.at[tile], idx_smem.at[s], idx_sem.at[s])

    def start_rows(s):
        for r in range(n_rows):
            row = pl.multiple_of(idx_smem[s, r] * SUBLANES, SUBLANES)
            pltpu.make_async_copy(y_hbm.at[pl.ds(row, SUBLANES)],
                                  ybuf.at[s, pl.ds(r * SUBLANES, SUBLANES)],
                                  row_sem.at[s]).start(priority=r % 2)

    def wait_rows(s):
        pltpu.make_async_copy(y_hbm.at[pl.ds(0, n_rows * SUBLANES)], ybuf.at[s],
                              row_sem.at[s]).wait()

    @pl.when(i == 0)
    def _():
        first = idx_copy(0, 0)
        first.start()
        first.wait()
        start_rows(0)
        idx_copy(nxt, 1).start()

    wait_rows(slot)
    idx_copy(nxt, 1 - slot).wait()
    start_rows(1 - slot)
    idx_copy(nxt2, slot).start()

    route = route_ref[...]
    lane = lax.broadcasted_iota(jnp.int32, route.shape, 1)
    w1 = jnp.sum(jnp.where(lane == 2, route, 0.0), axis=-1, keepdims=True)
    w2 = jnp.sum(jnp.where(lane == 3, route, 0.0), axis=-1, keepdims=True)
    y_ref = ybuf.at[slot]
    out = x_ref[...] + w1 * _load_token_tiles(y_ref, 0, tc) + w2 * _load_token_tiles(y_ref, tc, tc)
    o_ref[...] = _final_norm(out, gf_ref[...]) if final else out

    @pl.when(i == last)
    def _():
        wait_rows(1 - slot)
        idx_copy(nxt2, slot).wait()


def _combine(x, route, g_final, y_sorted, dest, final, tc):
    n, d = x.shape
    row = lambda width: pl.BlockSpec((tc, width), lambda i: (i, 0))
    return pl.pallas_call(
        functools.partial(_combine_kernel, final=final),
        out_shape=jax.ShapeDtypeStruct((n, d), F32),
        grid=(n // tc,),
        in_specs=[row(d), row(LANES), pl.BlockSpec((1, d), lambda i: (0, 0)),
                  pl.BlockSpec(memory_space=pl.ANY), pl.BlockSpec(memory_space=pl.ANY)],
        out_specs=row(d),
        scratch_shapes=[pltpu.VMEM((2, 2 * tc * SUBLANES, LANES), F32),
                        pltpu.SMEM((2, 2 * tc), jnp.int32),
                        pltpu.SemaphoreType.DMA((2,)), pltpu.SemaphoreType.DMA((2,))],
        compiler_params=_params("arbitrary"),
        name="expert_combine",
    )(x, route, g_final, y_sorted, dest)


def _routing_tables(route, tile_counts, n_experts, tm, tc):
    n = route.shape[0]
    n_router_tiles = tile_counts.shape[0]
    experts = jnp.arange(n_experts, dtype=jnp.int32)
    ids = route[:, :2].astype(jnp.int32).reshape(n_router_tiles, -1, 2)
    ranks = route[:, 4:6].astype(jnp.int32).reshape(n_router_tiles, -1, 2)
    counts = tile_counts.sum(axis=0)
    group = (counts + tm - 1) // tm * tm
    group_end = jnp.cumsum(group)
    tile_base = (group_end - group)[None, :] + jnp.cumsum(tile_counts, axis=0) - tile_counts
    chosen = ids[..., None] == experts
    dest = (jnp.sum(jnp.where(chosen, tile_base[:, None, None, :], 0), axis=-1) + ranks).reshape(n, 2)
    n_tiles = (2 * n) // tm + n_experts
    tile_start = jnp.arange(n_tiles, dtype=jnp.int32) * tm
    tile_expert = jnp.minimum(jnp.searchsorted(group_end, tile_start, side="right"),
                              n_experts - 1).astype(jnp.int32)
    dest_tiles = dest.reshape(n // tc, tc, 2).transpose(0, 2, 1).reshape(n // tc, 2 * tc)
    n_rows = n_tiles * tm
    pad_windows = jnp.minimum(group_end - group + counts, n_rows - tm)
    tail_windows = n_rows - tm * jnp.arange(1, n_experts + 1, dtype=jnp.int32)
    fill_start = jnp.concatenate([pad_windows, tail_windows]).astype(jnp.int32)
    return n_rows, tile_expert, dest_tiles, fill_start


def _pick_tile(n, want):
    tm = min(n, want)
    assert n % tm == 0 and tm % BLK == 0, (n, tm)
    return tm


def kernel(x, w_in, w_out, g_mix, g_ffn, g_sgu, sgu_w, sgu_b, g_out, ffn_w_gate, ffn_w_up,
           ffn_w_down, router_w, moe_w_gate, moe_w_up, moe_w_down, g_final):
    batch, seq, d = x.shape
    n = batch * seq
    depth = w_in.shape[0]
    n_groups = sgu_w.shape[1]
    d_sgu = n_groups * HEAD_DIM
    d_in = w_in.shape[2]
    d_sb = (d_in - 2 * d_sgu) // 3
    n_experts = router_w.shape[2]
    assert seq % BLK == 0 and sgu_w.shape[2] == BLK and d_sb % LANES == 0 and d_sgu % LANES == 0
    assert 3 * d_sb % d_sgu == 0 and d % LANES == 0

    col_scale = jnp.where(jnp.arange(d_in) < d_sb, HEAD_DIM ** -0.5, 1.0).astype(F32)[None, :]
    row2 = lambda a: a.reshape(1, -1).astype(F32)
    g_fin = row2(g_final)

    xf = x.reshape(n, d)
    for l in range(depth):
        last = l == depth - 1
        proj = _in_proj(xf, row2(g_mix[l]), col_scale, w_in[l].astype(BF16), _pick_tile(n, 512))
        y_sb = _attention(proj, row2(g_out[l, :d_sb]), batch, seq, d_sb)
        bias = jnp.repeat(sgu_b[l].T, HEAD_DIM, axis=1)
        y_sgu = _sgu(proj, sgu_w[l], bias, row2(g_sgu[l]), row2(g_out[l, d_sb:]), d_sb, d_sgu,
                     _pick_tile(n, 512))
        i = l // 2
        if l % 2 == 0:
            xf, h2 = _out_proj(y_sb, y_sgu, xf, w_out[l].astype(BF16), row2(g_ffn[l]), None, 0,
                               _pick_tile(n, 512))
            d_ff = ffn_w_gate.shape[2]
            assert d_ff % FF_CHUNK == 0
            nc = d_ff // FF_CHUNK
            wg = ffn_w_gate[i].astype(BF16).reshape(d, nc, FF_CHUNK).transpose(1, 0, 2)
            wu = ffn_w_up[i].astype(BF16).reshape(d, nc, FF_CHUNK).transpose(1, 0, 2)
            wd = ffn_w_down[i].astype(BF16).reshape(nc, FF_CHUNK, d)
            xf = _ffn(xf, h2, wg, wu, wd, g_fin, last, _pick_tile(n, 512))
        else:
            rw = jnp.pad(router_w[i].astype(F32), ((0, 0), (0, LANES - n_experts)))
            xf, h2, route, counts = _out_proj(y_sb, y_sgu, xf, w_out[l].astype(BF16),
                                              row2(g_ffn[l]), rw, n_experts, _pick_tile(n, 512))
            tile_counts = counts[::SUBLANES, :n_experts].astype(jnp.int32)
            tm, tc = _pick_tile(2 * n, MOE_ROW_TILE), _pick_tile(n, MOE_TOKEN_TILE)
            n_rows, tile_expert, dest, fill_start = _routing_tables(route, tile_counts, n_experts,
                                                                    tm, tc)
            x_sorted = _dispatch(h2, dest, fill_start, n_rows, tm, tc)
            y_sorted = _grouped_ffn(x_sorted, tile_expert, moe_w_gate[i].astype(BF16),
                                    moe_w_up[i].astype(BF16), moe_w_down[i].astype(BF16), tm)
            xf = _combine(xf, route, g_fin, y_sorted, dest, last, tc)
    return xf.reshape(batch, seq, d)
```

```python
import functools

import jax
import jax.numpy as jnp
from jax import lax
from jax.experimental import pallas as pl
from jax.experimental.pallas import tpu as pltpu

HEAD_DIM = 64
BLK = 128
LANES = 128
SUBLANES = 8
EPS = 1e-6
LOG2E = 1.4426950408889634
UNDERFLOW_LOG = -104.0
FF_CHUNK = 256
MOE_ROW_TILE = 512
MOE_TOKEN_TILE = 512
VMEM_LIMIT = 56 * 1024 * 1024

F32 = jnp.float32
BF16 = jnp.bfloat16


def _params(*sem):
    return pltpu.CompilerParams(dimension_semantics=sem, vmem_limit_bytes=VMEM_LIMIT)


def _split_bf16(a):
    hi = a.astype(BF16)
    lo = (a - hi.astype(F32)).astype(BF16)
    return hi, lo


def _same_head_mean_matrix():
    r = lax.broadcasted_iota(jnp.int32, (LANES, LANES), 0)
    c = lax.broadcasted_iota(jnp.int32, (LANES, LANES), 1)
    return jnp.where((r < HEAD_DIM) == (c < HEAD_DIM), 1.0 / HEAD_DIM, 0.0).astype(BF16)


def _head_rms_norm_many(ys, mean_mat, gain):
    mean2 = jnp.concatenate([mean_mat, mean_mat], axis=0)
    splits = [jnp.concatenate(_split_bf16(y * y), axis=1) for y in ys]
    means = [jnp.dot(s, mean2, preferred_element_type=F32) for s in splits]
    gains = gain if isinstance(gain, (list, tuple)) else [gain] * len(ys)
    return [y * lax.rsqrt(ms + EPS) * g for y, ms, g in zip(ys, means, gains)]


def _head_rms_norm(y, mean_mat, gain):
    return _head_rms_norm_many([y], mean_mat, gain)[0]


def _in_proj_sgu_kernel(x_ref, g_ref, cs_ref, w_ref, sw_ref, b_ref, gs_ref, go_ref, qkv_ref, y_ref):
    tm = x_ref.shape[0]
    d_qkv, d_sgu = qkv_ref.shape[1], y_ref.shape[1]
    x = x_ref[...]
    ms = jnp.mean(x * x, axis=-1, keepdims=True)
    h = (x * lax.rsqrt(ms + EPS) * g_ref[...]).astype(BF16)
    project = lambda lo, hi: jnp.dot(h, w_ref[:, lo:hi], preferred_element_type=F32)

    def qkv_part(c):
        cols = slice(c * d_sgu, (c + 1) * d_sgu)
        qkv_ref[:, cols] = (project(cols.start, cols.stop) * cs_ref[:, cols]).astype(qkv_ref.dtype)

    row = lax.broadcasted_iota(jnp.int32, (BLK, BLK), 0)
    col = lax.broadcasted_iota(jnp.int32, (BLK, BLK), 1)
    tril = col <= row
    head0 = col < HEAD_DIM
    mean_mat = _same_head_mean_matrix()
    blocks = [(slice(c * BLK, (c + 1) * BLK), slice(p * LANES, (p + 1) * LANES), p)
              for p in range(d_sgu // LANES) for c in range(tm // BLK)]
    mixers = [jnp.concatenate([jnp.where(tril, sw_ref[2 * p], 0.0),
                               jnp.where(tril, sw_ref[2 * p + 1], 0.0)], axis=0).astype(BF16)
              for p in range(d_sgu // LANES)]

    u = project(d_qkv, d_qkv + d_sgu)
    gate = project(d_qkv + d_sgu, d_qkv + 2 * d_sgu)
    gates = [jax.nn.gelu(gate[rs, ls]) for rs, ls, _ in blocks]
    gate_sq = [jnp.concatenate(_split_bf16(g * g), axis=1) for g in gates]
    qkv_part(0)
    mean2 = jnp.concatenate([mean_mat, mean_mat], axis=0)
    gates = [(g * lax.rsqrt(jnp.dot(s, mean2, preferred_element_type=F32) + EPS)
              * gs_ref[:, ls]).astype(BF16)
             for g, s, (_, ls, _) in zip(gates, gate_sq, blocks)]
    qkv_part(1)
    mixes = [jnp.dot(mixers[p], g, preferred_element_type=F32) for g, (_, _, p) in zip(gates, blocks)]
    ys = [jax.nn.gelu(u[rs, ls]) * (jnp.where(head0, m[:BLK], m[BLK:]) + b_ref[:, ls])
          for m, (rs, ls, _) in zip(mixes, blocks)]
    y_sq = [jnp.concatenate(_split_bf16(y * y), axis=1) for y in ys]
    qkv_part(2)
    for y, s, (rs, ls, _) in zip(ys, y_sq, blocks):
        norm = lax.rsqrt(jnp.dot(s, mean2, preferred_element_type=F32) + EPS)
        y_ref[rs, ls] = (y * norm * go_ref[:, ls]).astype(y_ref.dtype)


def _in_proj_sgu(x, g, col_scale, w, w_s, bias, g_gate, g_o, d_qkv, d_sgu, tm):
    n, d = x.shape
    assert d_qkv == 3 * d_sgu and w.shape[1] == d_qkv + 2 * d_sgu
    const = lambda a: pl.BlockSpec(a.shape, lambda i: (0,) * a.ndim)
    return pl.pallas_call(
        _in_proj_sgu_kernel,
        out_shape=[jax.ShapeDtypeStruct((n, d_qkv), BF16), jax.ShapeDtypeStruct((n, d_sgu), BF16)],
        grid=(n // tm,),
        in_specs=[
            pl.BlockSpec((tm, d), lambda i: (i, 0)),
            const(g), const(col_scale),
            pl.BlockSpec(w.shape, lambda i: (0, 0), pipeline_mode=pl.Buffered(1)),
            const(w_s), const(bias), const(g_gate), const(g_o),
        ],
        out_specs=[pl.BlockSpec((tm, d_qkv), lambda i: (i, 0)),
                   pl.BlockSpec((tm, d_sgu), lambda i: (i, 0))],
        compiler_params=_params("parallel"),
        name="in_proj_gating",
    )(x, g, col_scale, w, w_s, bias, g_gate, g_o)


def _attn_kernel(q_ref, k_ref, v_ref, g_ref, o_ref, q2_ref, v0_ref, v1_ref, acc_ref, z_ref, a_ref,
                 *, seq, pairs):
    nq = seq // BLK
    row = lax.broadcasted_iota(jnp.int32, (BLK, BLK), 0)
    col = lax.broadcasted_iota(jnp.int32, (BLK, BLK), 1)
    upper = jnp.where(row > col, 1.0, 0.0).astype(BF16)
    upper2 = jnp.concatenate([upper, upper], axis=0)
    causal = jnp.concatenate([col < row, col < row], axis=0)
    mean_mat = _same_head_mean_matrix()
    lane = lax.broadcasted_iota(jnp.int32, (1, pairs * LANES), 1) % LANES
    m0 = jnp.where(lane < HEAD_DIM, 1.0, 0.0).astype(BF16)
    m1 = jnp.where(lane < HEAD_DIM, 0.0, 1.0).astype(BF16)

    v0_ref[...] = v_ref[...] * m0
    v1_ref[...] = v_ref[...] * m1
    for p in range(pairs):
        ls = slice(p * LANES, (p + 1) * LANES)
        q2_ref[p, :, :BLK, :] = (q_ref[:, ls] * m0[:, ls]).reshape(nq, BLK, LANES)
        q2_ref[p, :, BLK:, :] = (q_ref[:, ls] * m1[:, ls]).reshape(nq, BLK, LANES)

    lanes = [slice(p * LANES, (p + 1) * LANES) for p in range(pairs)]

    def scores(i, j, slot):
        ks = pl.multiple_of(j * BLK, BLK)
        for p in range(pairs):
            z_ref[slot, p] = lax.dot_general(q2_ref[p, i], k_ref[pl.ds(ks, BLK), lanes[p]],
                                             (((1,), (1,)), ((), ())),
                                             preferred_element_type=F32)

    def weighted_values(par, j):
        ks = pl.multiple_of(j * BLK, BLK)
        for p in range(pairs):
            acc_ref[par, p] += (
                jnp.dot(a_ref[par, p, :BLK], v0_ref[pl.ds(ks, BLK), lanes[p]],
                        preferred_element_type=F32)
                + jnp.dot(a_ref[par, p, BLK:], v1_ref[pl.ds(ks, BLK), lanes[p]],
                          preferred_element_type=F32))

    def weights(par, zs, carries, mask):
        log_betas, sps, splits = [], [], []
        for z in zs:
            e = jnp.exp2(jnp.abs(z) * (-LOG2E))
            sp = jnp.maximum(z, 0.0) + jnp.log(1.0 + e)
            log_betas.append(z - sp)
            if mask is not None:
                sp = jnp.where(mask, sp, 0.0)
            sps.append(sp)
            splits.append(jnp.concatenate(_split_bf16(sp), axis=1))
        new_carries = []
        for p in range(pairs):
            rowsum = jnp.sum(sps[p], axis=-1, keepdims=True)
            new_carries.append(rowsum if carries is None else carries[p] + rowsum)
        if carries is None:
            alive = jnp.int32(1)
        else:
            low = functools.reduce(jnp.minimum, new_carries)
            alive = (jnp.min(low) < -UNDERFLOW_LOG).astype(jnp.int32)
        betweens = [jnp.dot(s, upper2, preferred_element_type=F32) for s in splits]
        for p in range(pairs):
            between = betweens[p] if carries is None else betweens[p] + carries[p]
            a = jnp.exp(log_betas[p] - between)
            if mask is not None:
                a = jnp.where(mask, a, 0.0)
            a_ref[par, p] = a.astype(BF16)
        return alive, tuple(new_carries)

    def diagonal(i, par):
        acc_ref[par] = jnp.zeros(acc_ref.shape[1:], F32)
        scores(i, i, 0)
        scores(i, jnp.maximum(i - 1, 0), 1)

    def finish(i, par, last_block):
        weighted_values(par, last_block)
        qs = pl.multiple_of(i * BLK, BLK)
        normed = _head_rms_norm_many([acc_ref[par, p] for p in range(pairs)], mean_mat,
                                     [g_ref[:, ls] for ls in lanes])
        for p in range(pairs):
            o_ref[pl.ds(qs, BLK), lanes[p]] = normed[p].astype(o_ref.dtype)

    def q_block(i, pending):
        par = i % 2
        diagonal(i, par)
        finish(i - 1, 1 - par, pending)
        alive, carries = weights(par, [z_ref[0, p] for p in range(pairs)], None, causal)

        def more_keys(state):
            return jnp.logical_and(state[0] < i, state[1] > 0)

        def key_block(state):
            jj, cs = state[0], state[2]
            j = i - 1 - jj
            slot = (jj + 1) % 2
            zs = [z_ref[slot, p] for p in range(pairs)]
            weighted_values(par, j + 1)
            scores(i, jnp.maximum(j - 1, 0), 1 - slot)
            alive, cs = weights(par, zs, cs, None)
            return jj + 1, alive, cs

        done = lax.while_loop(more_keys, key_block, (jnp.int32(0), alive, carries))[0]
        return i - done

    diagonal(0, 0)
    weights(0, [z_ref[0, p] for p in range(pairs)], None, causal)
    pending = lax.fori_loop(1, nq, q_block, jnp.int32(0))
    finish(nq - 1, (nq - 1) % 2, pending)


def _attention(proj, g_sb, batch, seq, d_sb):
    n = proj.shape[0]
    pairs = d_sb // LANES
    blk = lambda off: pl.BlockSpec((seq, d_sb), lambda b: (b, off))
    return pl.pallas_call(
        functools.partial(_attn_kernel, seq=seq, pairs=pairs),
        out_shape=jax.ShapeDtypeStruct((n, d_sb), BF16),
        grid=(batch,),
        in_specs=[blk(0), blk(1), blk(2), pl.BlockSpec((1, d_sb), lambda b: (0, 0))],
        out_specs=pl.BlockSpec((seq, d_sb), lambda b: (b, 0)),
        scratch_shapes=[pltpu.VMEM((pairs, seq // BLK, 2 * BLK, LANES), BF16),
                        pltpu.VMEM((seq, d_sb), BF16),
                        pltpu.VMEM((seq, d_sb), BF16),
                        pltpu.VMEM((2, pairs, BLK, LANES), F32),
                        pltpu.VMEM((2, pairs, 2 * BLK, LANES), F32),
                        pltpu.VMEM((2, pairs, 2 * BLK, LANES), BF16)],
        compiler_params=_params("parallel"),
        name="stickbreak_attn",
    )(proj, proj, proj, g_sb)


def _top2_routes(logits_blocks, n_experts):
    shape = logits_blocks[0].shape
    assert shape == (BLK, LANES)
    lane = lax.broadcasted_iota(jnp.int32, shape, 1).astype(F32)
    neg = jnp.float32(-jnp.inf)
    first_lane = lambda hit: jnp.min(jnp.where(hit, lane, float(LANES)), axis=-1, keepdims=True)
    lgs = [jnp.where(lane < n_experts, lg, neg) for lg in logits_blocks]
    m1s = [jnp.max(lg, axis=-1, keepdims=True) for lg in lgs]
    i1s = [first_lane(lg == m1) for lg, m1 in zip(lgs, m1s)]
    lg2s = [jnp.where(lane == i1, neg, lg) for lg, i1 in zip(lgs, i1s)]
    m2s = [jnp.max(lg2, axis=-1, keepdims=True) for lg2 in lg2s]
    i2s = [first_lane(lg2 == m2) for lg2, m2 in zip(lg2s, m2s)]
    r = lax.broadcasted_iota(jnp.int32, shape, 0)
    c = lax.broadcasted_iota(jnp.int32, shape, 1)
    earlier = jnp.where(c < r, 1.0, 0.0).astype(BF16)
    ones = jnp.ones(shape, BF16)
    chosen = [jnp.where((lane == i1) | (lane == i2), 1.0, 0.0).astype(BF16)
              for i1, i2 in zip(i1s, i2s)]
    within = [jnp.dot(earlier, m, preferred_element_type=F32) for m in chosen]
    totals = [jnp.dot(ones, m, preferred_element_type=F32) for m in chosen]
    routes = []
    before = None
    for m1, m2, i1, i2, rank in zip(m1s, m2s, i1s, i2s, within):
        if before is not None:
            rank = rank + before
        e2 = jnp.exp(m2 - m1)
        w1 = 1.0 / (1.0 + e2)
        w2 = e2 / (1.0 + e2)
        r1 = jnp.sum(jnp.where(lane == i1, rank, 0.0), axis=-1, keepdims=True)
        r2 = jnp.sum(jnp.where(lane == i2, rank, 0.0), axis=-1, keepdims=True)
        routes.append(jnp.where(lane == 0.0, i1, 0.0) + jnp.where(lane == 1.0, i2, 0.0)
                      + jnp.where(lane == 2.0, w1, 0.0) + jnp.where(lane == 3.0, w2, 0.0)
                      + jnp.where(lane == 4.0, r1, 0.0) + jnp.where(lane == 5.0, r2, 0.0))
        total = totals[len(routes) - 1]
        before = total if before is None else before + total
    return routes, before


def _out_proj_kernel(*refs, n_experts):
    if n_experts:
        ysb_ref, ysgu_ref, x_ref, w_ref, g_ref, rw_ref, xo_ref, h_ref, gates_ref, cnt_ref = refs
    else:
        ysb_ref, ysgu_ref, x_ref, w_ref, g_ref, xo_ref, h_ref = refs
    y = jnp.concatenate([ysb_ref[...], ysgu_ref[...]], axis=1)
    xn = x_ref[...] + jnp.dot(y, w_ref[...], preferred_element_type=F32)
    xo_ref[...] = xn
    ms = jnp.mean(xn * xn, axis=-1, keepdims=True)
    h = xn * lax.rsqrt(ms + EPS) * g_ref[...]
    if n_experts:
        _store_token_tiles(h_ref, h)
    else:
        h_ref[...] = h.astype(h_ref.dtype)
    if n_experts:
        w_hi, w_lo = _split_bf16(rw_ref[...])
        w_both = jnp.concatenate([w_hi, w_lo], axis=1)
        blocks = [slice(r * BLK, (r + 1) * BLK) for r in range(h.shape[0] // BLK)]
        splits = [_split_bf16(h[rs]) for rs in blocks]
        firsts = [jnp.dot(hi, w_both, preferred_element_type=F32) for hi, _ in splits]
        seconds = [jnp.dot(lo, w_hi, preferred_element_type=F32) for _, lo in splits]
        routes, counts = _top2_routes(
            [a[:, :LANES] + a[:, LANES:] + b for a, b in zip(firsts, seconds)], n_experts)
        for rs, route in zip(blocks, routes):
            gates_ref[rs, :] = route
        cnt_ref[...] = counts[:SUBLANES, :]


def _out_proj(ysb, ysgu, x, w, g, router_w, n_experts, tm):
    n, d = x.shape
    d_half = ysb.shape[1]
    row_blk = lambda width: pl.BlockSpec((tm, width), lambda i: (i, 0))
    const = lambda shape: pl.BlockSpec(shape, lambda i: (0, 0))
    in_specs = [row_blk(d_half), row_blk(ysgu.shape[1]), row_blk(d), const(w.shape), const((1, d))]
    if n_experts:
        assert d == SUBLANES * LANES
        out_shape = [jax.ShapeDtypeStruct((n, d), F32),
                     jax.ShapeDtypeStruct((n * SUBLANES, LANES), F32)]
        out_specs = [row_blk(d), pl.BlockSpec((tm * SUBLANES, LANES), lambda i: (i, 0))]
    else:
        out_shape = [jax.ShapeDtypeStruct((n, d), F32), jax.ShapeDtypeStruct((n, d), BF16)]
        out_specs = [row_blk(d), row_blk(d)]
    args = [ysb, ysgu, x, w, g]
    if n_experts:
        in_specs.append(const(router_w.shape))
        out_shape += [jax.ShapeDtypeStruct((n, LANES), F32),
                      jax.ShapeDtypeStruct((n // tm * SUBLANES, LANES), F32)]
        out_specs += [row_blk(LANES), pl.BlockSpec((SUBLANES, LANES), lambda i: (i, 0))]
        args.append(router_w)
    return pl.pallas_call(
        functools.partial(_out_proj_kernel, n_experts=n_experts),
        out_shape=out_shape,
        grid=(n // tm,),
        in_specs=in_specs,
        out_specs=out_specs,
        compiler_params=_params("parallel"),
        name="out_proj_router" if n_experts else "out_proj",
    )(*args)


def _final_norm(x, gain):
    ms = jnp.mean(x * x, axis=-1, keepdims=True)
    return x * lax.rsqrt(ms + EPS) * gain


def _ffn_kernel(x_ref, h_ref, wg_ref, wu_ref, wd_ref, gf_ref, o_ref, *, final):
    h = h_ref[...]
    acc = x_ref[...]
    for c in range(wg_ref.shape[0]):
        g = jnp.dot(h, wg_ref[c], preferred_element_type=F32)
        u = jnp.dot(h, wu_ref[c], preferred_element_type=F32)
        act = (jax.nn.silu(g) * u).astype(BF16)
        acc = acc + jnp.dot(act, wd_ref[c], preferred_element_type=F32)
    o_ref[...] = _final_norm(acc, gf_ref[...]) if final else acc


def _ffn(x, h, wg, wu, wd, g_final, final, tm):
    n, d = x.shape
    const3 = lambda a: pl.BlockSpec(a.shape, lambda i: (0, 0, 0), pipeline_mode=pl.Buffered(1))
    return pl.pallas_call(
        functools.partial(_ffn_kernel, final=final),
        out_shape=jax.ShapeDtypeStruct((n, d), F32),
        grid=(n // tm,),
        in_specs=[
            pl.BlockSpec((tm, d), lambda i: (i, 0)),
            pl.BlockSpec((tm, d), lambda i: (i, 0)),
            const3(wg), const3(wu), const3(wd),
            pl.BlockSpec((1, d), lambda i: (0, 0)),
        ],
        out_specs=pl.BlockSpec((tm, d), lambda i: (i, 0)),
        compiler_params=_params("parallel"),
        name="dense_swiglu",
    )(x, h, wg, wu, wd, g_final)


def _ff_chunks(d_ff):
    starts = list(range(0, d_ff, FF_CHUNK))
    return [(s, min(FF_CHUNK, d_ff - s)) for s in starts]


def _store_token_tiles(ref, value):
    rows = value.shape[0]
    for k in range(SUBLANES):
        ref[pl.ds(k, rows, stride=SUBLANES), :] = value[:, k * LANES:(k + 1) * LANES]


def _load_token_tiles(ref, first, rows):
    return jnp.concatenate([ref[pl.ds(first * SUBLANES + k, rows, stride=SUBLANES), :]
                            for k in range(SUBLANES)], axis=1)


def _dispatch_kernel(fill_ref, h_ref, idx_hbm, xs_hbm, idx_smem, zero_buf, row_sem, idx_sem,
                     fill_sem):
    i = pl.program_id(0)
    last = pl.num_programs(0) - 1
    slot = i % 2
    nxt = jnp.minimum(i + 1, last)
    tc = h_ref.shape[0] // SUBLANES

    def idx_copy(tile, s):
        return pltpu.make_async_copy(idx_hbm.at[tile], idx_smem.at[s], idx_sem.at[s])

    @pl.when(i == 0)
    def _():
        idx_copy(0, 0).start()
        zero_buf[...] = jnp.zeros_like(zero_buf)
        for j in range(fill_ref.shape[0]):
            first_row = pl.multiple_of(fill_ref[j] * SUBLANES, SUBLANES)
            fill = pltpu.make_async_copy(zero_buf, xs_hbm.at[pl.ds(first_row, zero_buf.shape[0])],
                                         fill_sem)
            fill.start()
            fill.wait()

    idx_copy(i, slot).wait()
    idx_copy(nxt, 1 - slot).start()
    for r in range(tc):
        for k in range(2):
            row = pl.multiple_of(idx_smem[slot, k * tc + r] * SUBLANES, SUBLANES)
            pltpu.make_async_copy(h_ref.at[pl.ds(r * SUBLANES, SUBLANES)],
                                  xs_hbm.at[pl.ds(row, SUBLANES)], row_sem).start(priority=k)
    for _ in range(2):
        pltpu.make_async_copy(h_ref, xs_hbm.at[pl.ds(0, tc * SUBLANES)], row_sem).wait()

    @pl.when(i == last)
    def _():
        idx_copy(nxt, 1 - slot).wait()


def _dispatch(h_tiles, dest, fill_start, n_rows, tm, tc):
    n_tok = h_tiles.shape[0] // SUBLANES
    grid_spec = pltpu.PrefetchScalarGridSpec(
        num_scalar_prefetch=1,
        grid=(n_tok // tc,),
        in_specs=[pl.BlockSpec((tc * SUBLANES, LANES), lambda i, fill: (i, 0)),
                  pl.BlockSpec(memory_space=pl.ANY)],
        out_specs=pl.BlockSpec(memory_space=pl.ANY),
        scratch_shapes=[pltpu.SMEM((2, 2 * tc), jnp.int32),
                        pltpu.VMEM((tm * SUBLANES, LANES), F32),
                        pltpu.SemaphoreType.DMA, pltpu.SemaphoreType.DMA((2,)),
                        pltpu.SemaphoreType.DMA],
    )
    return pl.pallas_call(
        _dispatch_kernel,
        out_shape=jax.ShapeDtypeStruct((n_rows * SUBLANES, LANES), F32),
        grid_spec=grid_spec,
        compiler_params=_params("arbitrary"),
        name="expert_dispatch",
    )(fill_start, h_tiles, dest)


def _grouped_ffn_kernel(te_ref, xs_ref, wg_ref, wu_ref, wd_ref, y_ref):
    i = pl.program_id(0)
    used = te_ref[pl.num_programs(0)]

    @pl.when(i < used)
    def _():
        tm = xs_ref.shape[0] // SUBLANES
        x = _load_token_tiles(xs_ref, 0, tm).astype(BF16)
        acc = None
        for s, w in _ff_chunks(wg_ref.shape[2]):
            g = jnp.dot(x, wg_ref[0, :, s:s + w], preferred_element_type=F32)
            u = jnp.dot(x, wu_ref[0, :, s:s + w], preferred_element_type=F32)
            act = (jax.nn.silu(g) * u).astype(BF16)
            part = jnp.dot(act, wd_ref[0, s:s + w, :], preferred_element_type=F32)
            acc = part if acc is None else acc + part
        _store_token_tiles(y_ref, acc)

    @pl.when(i >= used)
    def _():
        y_ref[...] = jnp.zeros_like(y_ref)


def _grouped_ffn(xs, tile_expert, wg, wu, wd, tm):
    n_tiles = tile_expert.shape[0] - 1
    _, d, d_ff = wg.shape
    rows = pl.BlockSpec((tm * SUBLANES, LANES), lambda i, te: (i, 0))
    grid_spec = pltpu.PrefetchScalarGridSpec(
        num_scalar_prefetch=1,
        grid=(n_tiles,),
        in_specs=[
            rows,
            pl.BlockSpec((1, d, d_ff), lambda i, te: (te[i], 0, 0)),
            pl.BlockSpec((1, d, d_ff), lambda i, te: (te[i], 0, 0)),
            pl.BlockSpec((1, d_ff, d), lambda i, te: (te[i], 0, 0)),
        ],
        out_specs=rows,
    )
    return pl.pallas_call(
        _grouped_ffn_kernel,
        out_shape=jax.ShapeDtypeStruct(xs.shape, F32),
        grid_spec=grid_spec,
        compiler_params=_params("arbitrary"),
        name="expert_swiglu",
    )(tile_expert, xs, wg, wu, wd)


def _combine_kernel(x_ref, route_ref, gf_ref, y_hbm, idx_hbm, o_ref,
                    ybuf, idx_smem, row_sem, idx_sem, *, final):
    tc = o_ref.shape[0]
    n_rows = 2 * tc
    i = pl.program_id(0)
    last = pl.num_programs(0) - 1
    slot = i % 2
    nxt = jnp.minimum(i + 1, last)
    nxt2 = jnp.minimum(i + 2, last)

    def idx_copy(tile, s):
        return pltpu.make_async_copy(idx_hbm.at[tile], idx_smem.at[s], idx_sem.at[s])

    def start_rows(s):
        for r in range(n_rows):
            row = pl.multiple_of(idx_smem[s, r] * SUBLANES, SUBLANES)
            pltpu.make_async_copy(y_hbm.at[pl.ds(row, SUBLANES)],
                                  ybuf.at[s, pl.ds(r * SUBLANES, SUBLANES)],
                                  row_sem.at[s]).start(priority=r % 2)

    def wait_rows(s):
        pltpu.make_async_copy(y_hbm.at[pl.ds(0, n_rows * SUBLANES)], ybuf.at[s],
                              row_sem.at[s]).wait()

    @pl.when(i == 0)
    def _():
        first = idx_copy(0, 0)
        first.start()
        first.wait()
        start_rows(0)
        idx_copy(nxt, 1).start()

    wait_rows(slot)
    idx_copy(nxt, 1 - slot).wait()
    start_rows(1 - slot)
    idx_copy(nxt2, slot).start()

    route = route_ref[...]
    lane = lax.broadcasted_iota(jnp.int32, route.shape, 1)
    w1 = jnp.sum(jnp.where(lane == 2, route, 0.0), axis=-1, keepdims=True)
    w2 = jnp.sum(jnp.where(lane == 3, route, 0.0), axis=-1, keepdims=True)
    y_ref = ybuf.at[slot]
    out = x_ref[...] + w1 * _load_token_tiles(y_ref, 0, tc) + w2 * _load_token_tiles(y_ref, tc, tc)
    o_ref[...] = _final_norm(out, gf_ref[...]) if final else out

    @pl.when(i == last)
    def _():
        wait_rows(1 - slot)
        idx_copy(nxt2, slot).wait()


def _combine(x, route, g_final, y_sorted, dest, final, tc):
    n, d = x.shape
    row = lambda width: pl.BlockSpec((tc, width), lambda i: (i, 0))
    return pl.pallas_call(
        functools.partial(_combine_kernel, final=final),
        out_shape=jax.ShapeDtypeStruct((n, d), F32),
        grid=(n // tc,),
        in_specs=[row(d), row(LANES), pl.BlockSpec((1, d), lambda i: (0, 0)),
                  pl.BlockSpec(memory_space=pl.ANY), pl.BlockSpec(memory_space=pl.ANY)],
        out_specs=row(d),
        scratch_shapes=[pltpu.VMEM((2, 2 * tc * SUBLANES, LANES), F32),
                        pltpu.SMEM((2, 2 * tc), jnp.int32),
                        pltpu.SemaphoreType.DMA((2,)), pltpu.SemaphoreType.DMA((2,))],
        compiler_params=_params("arbitrary"),
        name="expert_combine",
    )(x, route, g_final, y_sorted, dest)


def _routing_tables(route, tile_counts, n_experts, tm, tc):
    n = route.shape[0]
    n_router_tiles = tile_counts.shape[0]
    experts = jnp.arange(n_experts, dtype=jnp.int32)
    ids = route[:, :2].astype(jnp.int32).reshape(n_router_tiles, -1, 2)
    ranks = route[:, 4:6].astype(jnp.int32).reshape(n_router_tiles, -1, 2)
    counts = tile_counts.sum(axis=0)
    group = (counts + tm - 1) // tm * tm
    group_end = jnp.cumsum(group)
    tile_base = (group_end - group)[None, :] + jnp.cumsum(tile_counts, axis=0) - tile_counts
    chosen = ids[..., None] == experts
    dest = (jnp.sum(jnp.where(chosen, tile_base[:, None, None, :], 0), axis=-1) + ranks).reshape(n, 2)
    n_tiles = (2 * n) // tm + n_experts
    tile_start = jnp.arange(n_tiles, dtype=jnp.int32) * tm
    tile_expert = jnp.minimum(jnp.sum(tile_start[:, None] >= group_end[None, :], axis=1),
                              n_experts - 1).astype(jnp.int32)
    tile_expert = jnp.concatenate([tile_expert, (group_end[-1:] // tm).astype(jnp.int32)])
    dest_tiles = dest.reshape(n // tc, tc, 2).transpose(0, 2, 1).reshape(n // tc, 2 * tc)
    n_rows = n_tiles * tm
    pad_windows = jnp.minimum(group_end - group + counts, n_rows - tm)
    tail_windows = n_rows - tm * jnp.arange(1, n_experts + 1, dtype=jnp.int32)
    fill_start = jnp.concatenate([pad_windows, tail_windows]).astype(jnp.int32)
    return n_rows, tile_expert, dest_tiles, fill_start


def _pick_tile(n, want):
    tm = min(n, want)
    assert n % tm == 0 and tm % BLK == 0, (n, tm)
    return tm


def kernel(x, w_in, w_out, g_mix, g_ffn, g_sgu, sgu_w, sgu_b, g_out, ffn_w_gate, ffn_w_up,
           ffn_w_down, router_w, moe_w_gate, moe_w_up, moe_w_down, g_final):
    batch, seq, d = x.shape
    n = batch * seq
    depth = w_in.shape[0]
    n_groups = sgu_w.shape[1]
    d_sgu = n_groups * HEAD_DIM
    d_in = w_in.shape[2]
    d_sb = (d_in - 2 * d_sgu) // 3
    n_experts = router_w.shape[2]
    assert seq % BLK == 0 and sgu_w.shape[2] == BLK and d_sb % LANES == 0 and d_sgu % LANES == 0
    assert 3 * d_sb % d_sgu == 0 and d % LANES == 0

    col_scale = jnp.where(jnp.arange(d_in) < d_sb, HEAD_DIM ** -0.5, 1.0).astype(F32)[None, :]
    row2 = lambda a: a.reshape(1, -1).astype(F32)
    g_fin = row2(g_final)

    xf = x.reshape(n, d)
    for l in range(depth):
        last = l == depth - 1
        bias = jnp.repeat(sgu_b[l].T, HEAD_DIM, axis=1)
        qkv, y_sgu = _in_proj_sgu(xf, row2(g_mix[l]), col_scale, w_in[l].astype(BF16), sgu_w[l],
                                  bias, row2(g_sgu[l]), row2(g_out[l, d_sb:]), 3 * d_sb, d_sgu,
                                  _pick_tile(n, 512))
        y_sb = _attention(qkv, row2(g_out[l, :d_sb]), batch, seq, d_sb)
        i = l // 2
        if l % 2 == 0:
            xf, h2 = _out_proj(y_sb, y_sgu, xf, w_out[l].astype(BF16), row2(g_ffn[l]), None, 0,
                               _pick_tile(n, 512))
            d_ff = ffn_w_gate.shape[2]
            assert d_ff % FF_CHUNK == 0
            nc = d_ff // FF_CHUNK
            wg = ffn_w_gate[i].astype(BF16).reshape(d, nc, FF_CHUNK).transpose(1, 0, 2)
            wu = ffn_w_up[i].astype(BF16).reshape(d, nc, FF_CHUNK).transpose(1, 0, 2)
            wd = ffn_w_down[i].astype(BF16).reshape(nc, FF_CHUNK, d)
            xf = _ffn(xf, h2, wg, wu, wd, g_fin, last, _pick_tile(n, 512))
        else:
            rw = jnp.pad(router_w[i].astype(F32), ((0, 0), (0, LANES - n_experts)))
            xf, h2, route, counts = _out_proj(y_sb, y_sgu, xf, w_out[l].astype(BF16),
                                              row2(g_ffn[l]), rw, n_experts, _pick_tile(n, 512))
            tile_counts = counts[::SUBLANES, :n_experts].astype(jnp.int32)
            tm, tc = _pick_tile(2 * n, MOE_ROW_TILE), _pick_tile(n, MOE_TOKEN_TILE)
            n_rows, tile_expert, dest, fill_start = _routing_tables(route, tile_counts, n_experts,
                                                                    tm, tc)
            x_sorted = _dispatch(h2, dest, fill_start, n_rows, tm, tc)
            y_sorted = _grouped_ffn(x_sorted, tile_expert, moe_w_gate[i].astype(BF16),
                                    moe_w_up[i].astype(BF16), moe_w_down[i].astype(BF16), tm)
            xf = _combine(xf, route, g_fin, y_sorted, dest, last, tc)
    return xf.reshape(batch, seq, d)
```

```python
import functools

import jax
import jax.numpy as jnp
from jax import lax
from jax.experimental import pallas as pl
from jax.experimental.pallas import tpu as pltpu

HEAD_DIM = 64
BLK = 128
LANES = 128
SUBLANES = 8
EPS = 1e-6
LOG2E = 1.4426950408889634
UNDERFLOW_LOG = -104.0
FF_CHUNK = 256
MOE_ROW_TILE = 512
MOE_TOKEN_TILE = 512
VMEM_LIMIT = 56 * 1024 * 1024

F32 = jnp.float32
BF16 = jnp.bfloat16


def _params(*sem):
    return pltpu.CompilerParams(dimension_semantics=sem, vmem_limit_bytes=VMEM_LIMIT)


def _split_bf16(a):
    hi = a.astype(BF16)
    lo = (a - hi.astype(F32)).astype(BF16)
    return hi, lo


def _same_head_mean_matrix():
    r = lax.broadcasted_iota(jnp.int32, (LANES, LANES), 0)
    c = lax.broadcasted_iota(jnp.int32, (LANES, LANES), 1)
    return jnp.where((r < HEAD_DIM) == (c < HEAD_DIM), 1.0 / HEAD_DIM, 0.0).astype(BF16)


def _head_rms_norm_many(ys, mean_mat, gain):
    mean2 = jnp.concatenate([mean_mat, mean_mat], axis=0)
    splits = [jnp.concatenate(_split_bf16(y * y), axis=1) for y in ys]
    means = [jnp.dot(s, mean2, preferred_element_type=F32) for s in splits]
    gains = gain if isinstance(gain, (list, tuple)) else [gain] * len(ys)
    return [y * lax.rsqrt(ms + EPS) * g for y, ms, g in zip(ys, means, gains)]


def _head_rms_norm(y, mean_mat, gain):
    return _head_rms_norm_many([y], mean_mat, gain)[0]


def _in_proj_sgu_kernel(x_ref, g_ref, cs_ref, w_ref, sw_ref, b_ref, gs_ref, go_ref, qkv_ref, y_ref):
    tm = x_ref.shape[0]
    d_qkv, d_sgu = qkv_ref.shape[1], y_ref.shape[1]
    x = x_ref[...]
    ms = jnp.mean(x * x, axis=-1, keepdims=True)
    h = (x * lax.rsqrt(ms + EPS) * g_ref[...]).astype(BF16)
    project = lambda lo, hi: jnp.dot(h, w_ref[:, lo:hi], preferred_element_type=F32)

    def qkv_part(c):
        cols = slice(c * d_sgu, (c + 1) * d_sgu)
        qkv_ref[:, cols] = (project(cols.start, cols.stop) * cs_ref[:, cols]).astype(qkv_ref.dtype)

    row = lax.broadcasted_iota(jnp.int32, (BLK, BLK), 0)
    col = lax.broadcasted_iota(jnp.int32, (BLK, BLK), 1)
    tril = col <= row
    head0 = col < HEAD_DIM
    mean_mat = _same_head_mean_matrix()
    blocks = [(slice(c * BLK, (c + 1) * BLK), slice(p * LANES, (p + 1) * LANES), p)
              for p in range(d_sgu // LANES) for c in range(tm // BLK)]
    mixers = [jnp.concatenate([jnp.where(tril, sw_ref[2 * p], 0.0),
                               jnp.where(tril, sw_ref[2 * p + 1], 0.0)], axis=0).astype(BF16)
              for p in range(d_sgu // LANES)]

    u = project(d_qkv, d_qkv + d_sgu)
    gate = project(d_qkv + d_sgu, d_qkv + 2 * d_sgu)
    gates = [jax.nn.gelu(gate[rs, ls]) for rs, ls, _ in blocks]
    gate_sq = [jnp.concatenate(_split_bf16(g * g), axis=1) for g in gates]
    qkv_part(0)
    mean2 = jnp.concatenate([mean_mat, mean_mat], axis=0)
    gates = [(g * lax.rsqrt(jnp.dot(s, mean2, preferred_element_type=F32) + EPS)
              * gs_ref[:, ls]).astype(BF16)
             for g, s, (_, ls, _) in zip(gates, gate_sq, blocks)]
    qkv_part(1)
    mixes = [jnp.dot(mixers[p], g, preferred_element_type=F32) for g, (_, _, p) in zip(gates, blocks)]
    ys = [jax.nn.gelu(u[rs, ls]) * (jnp.where(head0, m[:BLK], m[BLK:]) + b_ref[:, ls])
          for m, (rs, ls, _) in zip(mixes, blocks)]
    y_sq = [jnp.concatenate(_split_bf16(y * y), axis=1) for y in ys]
    qkv_part(2)
    for y, s, (rs, ls, _) in zip(ys, y_sq, blocks):
        norm = lax.rsqrt(jnp.dot(s, mean2, preferred_element_type=F32) + EPS)
        y_ref[rs, ls] = (y * norm * go_ref[:, ls]).astype(y_ref.dtype)


def _in_proj_sgu(x, g, col_scale, w, w_s, bias, g_gate, g_o, d_qkv, d_sgu, tm):
    n, d = x.shape
    assert d_qkv == 3 * d_sgu and w.shape[1] == d_qkv + 2 * d_sgu
    const = lambda a: pl.BlockSpec(a.shape, lambda i: (0,) * a.ndim)
    return pl.pallas_call(
        _in_proj_sgu_kernel,
        out_shape=[jax.ShapeDtypeStruct((n, d_qkv), BF16), jax.ShapeDtypeStruct((n, d_sgu), BF16)],
        grid=(n // tm,),
        in_specs=[
            pl.BlockSpec((tm, d), lambda i: (i, 0)),
            const(g), const(col_scale),
            pl.BlockSpec(w.shape, lambda i: (0, 0), pipeline_mode=pl.Buffered(1)),
            const(w_s), const(bias), const(g_gate), const(g_o),
        ],
        out_specs=[pl.BlockSpec((tm, d_qkv), lambda i: (i, 0)),
                   pl.BlockSpec((tm, d_sgu), lambda i: (i, 0))],
        compiler_params=_params("parallel"),
        name="in_proj_gating",
    )(x, g, col_scale, w, w_s, bias, g_gate, g_o)


def _attn_kernel(q_ref, k_ref, v_ref, g_ref, o_ref, q2_ref, v0_ref, v1_ref, acc_ref, z_ref, a_ref,
                 *, seq, pairs):
    nq = seq // BLK
    row = lax.broadcasted_iota(jnp.int32, (BLK, BLK), 0)
    col = lax.broadcasted_iota(jnp.int32, (BLK, BLK), 1)
    upper = jnp.where(row > col, 1.0, 0.0).astype(BF16)
    causal = jnp.concatenate([col < row, col < row], axis=0)
    mean_mat = _same_head_mean_matrix()
    lane = lax.broadcasted_iota(jnp.int32, (1, pairs * LANES), 1) % LANES
    m0 = jnp.where(lane < HEAD_DIM, 1.0, 0.0).astype(BF16)
    m1 = jnp.where(lane < HEAD_DIM, 0.0, 1.0).astype(BF16)

    v0_ref[...] = v_ref[...] * m0
    v1_ref[...] = v_ref[...] * m1
    for p in range(pairs):
        ls = slice(p * LANES, (p + 1) * LANES)
        q2_ref[p, :, :BLK, :] = (q_ref[:, ls] * m0[:, ls]).reshape(nq, BLK, LANES)
        q2_ref[p, :, BLK:, :] = (q_ref[:, ls] * m1[:, ls]).reshape(nq, BLK, LANES)

    lanes = [slice(p * LANES, (p + 1) * LANES) for p in range(pairs)]

    def scores(i, j, slot):
        ks = pl.multiple_of(j * BLK, BLK)
        for p in range(pairs):
            z_ref[slot, p] = lax.dot_general(q2_ref[p, i], k_ref[pl.ds(ks, BLK), lanes[p]],
                                             (((1,), (1,)), ((), ())),
                                             preferred_element_type=F32)

    def weighted_values(par, j):
        ks = pl.multiple_of(j * BLK, BLK)
        for p in range(pairs):
            acc_ref[par, p] += (
                jnp.dot(a_ref[par, p, :BLK], v0_ref[pl.ds(ks, BLK), lanes[p]],
                        preferred_element_type=F32)
                + jnp.dot(a_ref[par, p, BLK:], v1_ref[pl.ds(ks, BLK), lanes[p]],
                          preferred_element_type=F32))

    def weights(par, zs, carries, mask):
        log_betas, sps, splits = [], [], []
        for z in zs:
            e = jnp.exp2(jnp.abs(z) * (-LOG2E))
            sp = jnp.maximum(z, 0.0) + jnp.log(1.0 + e)
            log_betas.append(z - sp)
            if mask is not None:
                sp = jnp.where(mask, sp, 0.0)
            sps.append(sp)
            splits.append(sp.astype(BF16))
        new_carries = []
        for p in range(pairs):
            rowsum = jnp.sum(sps[p], axis=-1, keepdims=True)
            new_carries.append(rowsum if carries is None else carries[p] + rowsum)
        if carries is None:
            alive = jnp.int32(1)
        else:
            low = functools.reduce(jnp.minimum, new_carries)
            alive = (jnp.min(low) < -UNDERFLOW_LOG).astype(jnp.int32)
        betweens = [jnp.dot(s, upper, preferred_element_type=F32) for s in splits]
        for p in range(pairs):
            between = betweens[p] if carries is None else betweens[p] + carries[p]
            a = jnp.exp(log_betas[p] - between)
            if mask is not None:
                a = jnp.where(mask, a, 0.0)
            a_ref[par, p] = a.astype(BF16)
        return alive, tuple(new_carries)

    def diagonal(i, par):
        acc_ref[par] = jnp.zeros(acc_ref.shape[1:], F32)
        scores(i, i, 0)
        scores(i, jnp.maximum(i - 1, 0), 1)

    def finish(i, par, last_block):
        weighted_values(par, last_block)
        qs = pl.multiple_of(i * BLK, BLK)
        normed = _head_rms_norm_many([acc_ref[par, p] for p in range(pairs)], mean_mat,
                                     [g_ref[:, ls] for ls in lanes])
        for p in range(pairs):
            o_ref[pl.ds(qs, BLK), lanes[p]] = normed[p].astype(o_ref.dtype)

    def q_block(i, pending):
        par = i % 2
        diagonal(i, par)
        finish(i - 1, 1 - par, pending)
        alive, carries = weights(par, [z_ref[0, p] for p in range(pairs)], None, causal)

        def more_keys(state):
            return jnp.logical_and(state[0] < i, state[1] > 0)

        def key_block(state):
            jj, cs = state[0], state[2]
            j = i - 1 - jj
            slot = (jj + 1) % 2
            zs = [z_ref[slot, p] for p in range(pairs)]
            weighted_values(par, j + 1)
            scores(i, jnp.maximum(j - 1, 0), 1 - slot)
            alive, cs = weights(par, zs, cs, None)
            return jj + 1, alive, cs

        done = lax.while_loop(more_keys, key_block, (jnp.int32(0), alive, carries))[0]
        return i - done

    diagonal(0, 0)
    weights(0, [z_ref[0, p] for p in range(pairs)], None, causal)
    pending = lax.fori_loop(1, nq, q_block, jnp.int32(0))
    finish(nq - 1, (nq - 1) % 2, pending)


def _attention(proj, g_sb, batch, seq, d_sb):
    n = proj.shape[0]
    pairs = d_sb // LANES
    blk = lambda off: pl.BlockSpec((seq, d_sb), lambda b: (b, off))
    return pl.pallas_call(
        functools.partial(_attn_kernel, seq=seq, pairs=pairs),
        out_shape=jax.ShapeDtypeStruct((n, d_sb), BF16),
        grid=(batch,),
        in_specs=[blk(0), blk(1), blk(2), pl.BlockSpec((1, d_sb), lambda b: (0, 0))],
        out_specs=pl.BlockSpec((seq, d_sb), lambda b: (b, 0)),
        scratch_shapes=[pltpu.VMEM((pairs, seq // BLK, 2 * BLK, LANES), BF16),
                        pltpu.VMEM((seq, d_sb), BF16),
                        pltpu.VMEM((seq, d_sb), BF16),
                        pltpu.VMEM((2, pairs, BLK, LANES), F32),
                        pltpu.VMEM((2, pairs, 2 * BLK, LANES), F32),
                        pltpu.VMEM((2, pairs, 2 * BLK, LANES), BF16)],
        compiler_params=_params("parallel"),
        name="stickbreak_attn",
    )(proj, proj, proj, g_sb)


def _top2_routes(logits_blocks, n_experts):
    shape = logits_blocks[0].shape
    assert shape == (BLK, LANES)
    lane = lax.broadcasted_iota(jnp.int32, shape, 1).astype(F32)
    neg = jnp.float32(-jnp.inf)
    first_lane = lambda hit: jnp.min(jnp.where(hit, lane, float(LANES)), axis=-1, keepdims=True)
    lgs = [jnp.where(lane < n_experts, lg, neg) for lg in logits_blocks]
    m1s = [jnp.max(lg, axis=-1, keepdims=True) for lg in lgs]
    i1s = [first_lane(lg == m1) for lg, m1 in zip(lgs, m1s)]
    lg2s = [jnp.where(lane == i1, neg, lg) for lg, i1 in zip(lgs, i1s)]
    m2s = [jnp.max(lg2, axis=-1, keepdims=True) for lg2 in lg2s]
    i2s = [first_lane(lg2 == m2) for lg2, m2 in zip(lg2s, m2s)]
    r = lax.broadcasted_iota(jnp.int32, shape, 0)
    c = lax.broadcasted_iota(jnp.int32, shape, 1)
    earlier = jnp.where(c < r, 1.0, 0.0).astype(BF16)
    ones = jnp.ones(shape, BF16)
    chosen = [jnp.where((lane == i1) | (lane == i2), 1.0, 0.0).astype(BF16)
              for i1, i2 in zip(i1s, i2s)]
    within = [jnp.dot(earlier, m, preferred_element_type=F32) for m in chosen]
    totals = [jnp.dot(ones, m, preferred_element_type=F32) for m in chosen]
    routes = []
    before = None
    for m1, m2, i1, i2, rank in zip(m1s, m2s, i1s, i2s, within):
        if before is not None:
            rank = rank + before
        e2 = jnp.exp(m2 - m1)
        w1 = 1.0 / (1.0 + e2)
        w2 = e2 / (1.0 + e2)
        r1 = jnp.sum(jnp.where(lane == i1, rank, 0.0), axis=-1, keepdims=True)
        r2 = jnp.sum(jnp.where(lane == i2, rank, 0.0), axis=-1, keepdims=True)
        routes.append(jnp.where(lane == 0.0, i1, 0.0) + jnp.where(lane == 1.0, i2, 0.0)
                      + jnp.where(lane == 2.0, w1, 0.0) + jnp.where(lane == 3.0, w2, 0.0)
                      + jnp.where(lane == 4.0, r1, 0.0) + jnp.where(lane == 5.0, r2, 0.0))
        total = totals[len(routes) - 1]
        before = total if before is None else before + total
    return routes, before


def _final_norm(x, gain):
    ms = jnp.mean(x * x, axis=-1, keepdims=True)
    return x * lax.rsqrt(ms + EPS) * gain


def _mix_residual(ysb_ref, ysgu_ref, x_ref, w_ref, g_ref):
    y = jnp.concatenate([ysb_ref[...], ysgu_ref[...]], axis=1)
    xn = x_ref[...] + jnp.dot(y, w_ref[...], preferred_element_type=F32)
    return xn, _final_norm(xn, g_ref[...])


def _out_proj_router_kernel(ysb_ref, ysgu_ref, x_ref, w_ref, g_ref, rw_ref,
                            xo_ref, h_ref, route_ref, cnt_ref, *, n_experts):
    xn, h = _mix_residual(ysb_ref, ysgu_ref, x_ref, w_ref, g_ref)
    xo_ref[...] = xn
    _store_token_tiles(h_ref, h)
    w_hi, w_lo = _split_bf16(rw_ref[...])
    w_both = jnp.concatenate([w_hi, w_lo], axis=1)
    blocks = [slice(r * BLK, (r + 1) * BLK) for r in range(h.shape[0] // BLK)]
    splits = [_split_bf16(h[rs]) for rs in blocks]
    firsts = [jnp.dot(hi, w_both, preferred_element_type=F32) for hi, _ in splits]
    seconds = [jnp.dot(lo, w_hi, preferred_element_type=F32) for _, lo in splits]
    routes, counts = _top2_routes(
        [a[:, :LANES] + a[:, LANES:] + b for a, b in zip(firsts, seconds)], n_experts)
    for rs, route in zip(blocks, routes):
        route_ref[rs, :] = route
    cnt_ref[...] = counts[:SUBLANES, :]


def _out_proj_router(ysb, ysgu, x, w, g, router_w, n_experts, tm):
    n, d = x.shape
    assert d == SUBLANES * LANES
    row_blk = lambda width: pl.BlockSpec((tm, width), lambda i: (i, 0))
    const = lambda a: pl.BlockSpec(a.shape, lambda i: (0, 0))
    return pl.pallas_call(
        functools.partial(_out_proj_router_kernel, n_experts=n_experts),
        out_shape=[jax.ShapeDtypeStruct((n, d), F32),
                   jax.ShapeDtypeStruct((n * SUBLANES, LANES), F32),
                   jax.ShapeDtypeStruct((n, LANES), F32),
                   jax.ShapeDtypeStruct((n // tm * SUBLANES, LANES), F32)],
        grid=(n // tm,),
        in_specs=[row_blk(ysb.shape[1]), row_blk(ysgu.shape[1]), row_blk(d), const(w), const(g),
                  const(router_w)],
        out_specs=[row_blk(d), pl.BlockSpec((tm * SUBLANES, LANES), lambda i: (i, 0)),
                   row_blk(LANES), pl.BlockSpec((SUBLANES, LANES), lambda i: (i, 0))],
        compiler_params=_params("parallel"),
        name="out_proj_router",
    )(ysb, ysgu, x, w, g, router_w)


def _out_proj_ffn_kernel(ysb_ref, ysgu_ref, x_ref, w_ref, g_ref, wg_ref, wu_ref, wd_ref, gf_ref,
                         o_ref, *, final):
    acc, h = _mix_residual(ysb_ref, ysgu_ref, x_ref, w_ref, g_ref)
    h = h.astype(BF16)
    for c in range(wg_ref.shape[0]):
        g = jnp.dot(h, wg_ref[c], preferred_element_type=F32)
        u = jnp.dot(h, wu_ref[c], preferred_element_type=F32)
        act = (jax.nn.silu(g) * u).astype(BF16)
        acc = acc + jnp.dot(act, wd_ref[c], preferred_element_type=F32)
    o_ref[...] = _final_norm(acc, gf_ref[...]) if final else acc


def _out_proj_ffn(ysb, ysgu, x, w, g, wg, wu, wd, g_final, final, tm):
    n, d = x.shape
    row_blk = lambda width: pl.BlockSpec((tm, width), lambda i: (i, 0))
    resident = lambda a: pl.BlockSpec(a.shape, lambda i: (0,) * a.ndim, pipeline_mode=pl.Buffered(1))
    return pl.pallas_call(
        functools.partial(_out_proj_ffn_kernel, final=final),
        out_shape=jax.ShapeDtypeStruct((n, d), F32),
        grid=(n // tm,),
        in_specs=[row_blk(ysb.shape[1]), row_blk(ysgu.shape[1]), row_blk(d), resident(w),
                  pl.BlockSpec((1, d), lambda i: (0, 0)),
                  resident(wg), resident(wu), resident(wd),
                  pl.BlockSpec((1, d), lambda i: (0, 0))],
        out_specs=row_blk(d),
        compiler_params=_params("parallel"),
        name="out_proj_dense_swiglu",
    )(ysb, ysgu, x, w, g, wg, wu, wd, g_final)


def _ff_chunks(d_ff):
    starts = list(range(0, d_ff, FF_CHUNK))
    return [(s, min(FF_CHUNK, d_ff - s)) for s in starts]


def _store_token_tiles(ref, value):
    rows = value.shape[0]
    for k in range(SUBLANES):
        ref[pl.ds(k, rows, stride=SUBLANES), :] = value[:, k * LANES:(k + 1) * LANES]


def _load_token_tiles(ref, first, rows):
    return jnp.concatenate([ref[pl.ds(first * SUBLANES + k, rows, stride=SUBLANES), :]
                            for k in range(SUBLANES)], axis=1)


def _dispatch_kernel(fill_ref, h_ref, idx_hbm, xs_hbm, idx_smem, zero_buf, row_sem, idx_sem,
                     fill_sem):
    i = pl.program_id(0)
    last = pl.num_programs(0) - 1
    slot = i % 2
    nxt = jnp.minimum(i + 1, last)
    tc = h_ref.shape[0] // SUBLANES

    def idx_copy(tile, s):
        return pltpu.make_async_copy(idx_hbm.at[tile], idx_smem.at[s], idx_sem.at[s])

    @pl.when(i == 0)
    def _():
        idx_copy(0, 0).start()
        zero_buf[...] = jnp.zeros_like(zero_buf)
        for j in range(fill_ref.shape[0]):
            first_row = pl.multiple_of(fill_ref[j] * SUBLANES, SUBLANES)
            fill = pltpu.make_async_copy(zero_buf, xs_hbm.at[pl.ds(first_row, zero_buf.shape[0])],
                                         fill_sem)
            fill.start()
            fill.wait()

    idx_copy(i, slot).wait()
    idx_copy(nxt, 1 - slot).start()
    for r in range(tc):
        for k in range(2):
            row = pl.multiple_of(idx_smem[slot, k * tc + r] * SUBLANES, SUBLANES)
            pltpu.make_async_copy(h_ref.at[pl.ds(r * SUBLANES, SUBLANES)],
                                  xs_hbm.at[pl.ds(row, SUBLANES)], row_sem).start(priority=k)
    for _ in range(2):
        pltpu.make_async_copy(h_ref, xs_hbm.at[pl.ds(0, tc * SUBLANES)], row_sem).wait()

    @pl.when(i == last)
    def _():
        idx_copy(nxt, 1 - slot).wait()


def _dispatch(h_tiles, dest, fill_start, n_rows, tm, tc):
    n_tok = h_tiles.shape[0] // SUBLANES
    grid_spec = pltpu.PrefetchScalarGridSpec(
        num_scalar_prefetch=1,
        grid=(n_tok // tc,),
        in_specs=[pl.BlockSpec((tc * SUBLANES, LANES), lambda i, fill: (i, 0)),
                  pl.BlockSpec(memory_space=pl.ANY)],
        out_specs=pl.BlockSpec(memory_space=pl.ANY),
        scratch_shapes=[pltpu.SMEM((2, 2 * tc), jnp.int32),
                        pltpu.VMEM((tm * SUBLANES, LANES), F32),
                        pltpu.SemaphoreType.DMA, pltpu.SemaphoreType.DMA((2,)),
                        pltpu.SemaphoreType.DMA],
    )
    return pl.pallas_call(
        _dispatch_kernel,
        out_shape=jax.ShapeDtypeStruct((n_rows * SUBLANES, LANES), F32),
        grid_spec=grid_spec,
        compiler_params=_params("arbitrary"),
        name="expert_dispatch",
    )(fill_start, h_tiles, dest)


def _grouped_ffn_kernel(te_ref, xs_ref, wg_ref, wu_ref, wd_ref, y_ref):
    i = pl.program_id(0)
    used = te_ref[pl.num_programs(0)]

    @pl.when(i < used)
    def _():
        tm = xs_ref.shape[0] // SUBLANES
        x = _load_token_tiles(xs_ref, 0, tm).astype(BF16)
        acc = None
        for s, w in _ff_chunks(wg_ref.shape[2]):
            g = jnp.dot(x, wg_ref[0, :, s:s + w], preferred_element_type=F32)
            u = jnp.dot(x, wu_ref[0, :, s:s + w], preferred_element_type=F32)
            act = (jax.nn.silu(g) * u).astype(BF16)
            part = jnp.dot(act, wd_ref[0, s:s + w, :], preferred_element_type=F32)
            acc = part if acc is None else acc + part
        _store_token_tiles(y_ref, acc)

    @pl.when(i >= used)
    def _():
        y_ref[...] = jnp.zeros_like(y_ref)


def _grouped_ffn(xs, tile_expert, wg, wu, wd, tm):
    n_tiles = tile_expert.shape[0] - 1
    _, d, d_ff = wg.shape
    rows = pl.BlockSpec((tm * SUBLANES, LANES), lambda i, te: (i, 0))
    grid_spec = pltpu.PrefetchScalarGridSpec(
        num_scalar_prefetch=1,
        grid=(n_tiles,),
        in_specs=[
            rows,
            pl.BlockSpec((1, d, d_ff), lambda i, te: (te[i], 0, 0)),
            pl.BlockSpec((1, d, d_ff), lambda i, te: (te[i], 0, 0)),
            pl.BlockSpec((1, d_ff, d), lambda i, te: (te[i], 0, 0)),
        ],
        out_specs=rows,
    )
    return pl.pallas_call(
        _grouped_ffn_kernel,
        out_shape=jax.ShapeDtypeStruct(xs.shape, F32),
        grid_spec=grid_spec,
        compiler_params=_params("arbitrary"),
        name="expert_swiglu",
    )(tile_expert, xs, wg, wu, wd)


def _combine_kernel(x_ref, route_ref, gf_ref, y_hbm, idx_hbm, o_ref,
                    ybuf, idx_smem, row_sem, idx_sem, *, final):
    tc = o_ref.shape[0]
    n_rows = 2 * tc
    i = pl.program_id(0)
    last = pl.num_programs(0) - 1
    slot = i % 2
    nxt = jnp.minimum(i + 1, last)
    nxt2 = jnp.minimum(i + 2, last)

    def idx_copy(tile, s):
        return pltpu.make_async_copy(idx_hbm.at[tile], idx_smem.at[s], idx_sem.at[s])

    def start_rows(s):
        for r in range(n_rows):
            row = pl.multiple_of(idx_smem[s, r] * SUBLANES, SUBLANES)
            pltpu.make_async_copy(y_hbm.at[pl.ds(row, SUBLANES)],
                                  ybuf.at[s, pl.ds(r * SUBLANES, SUBLANES)],
                                  row_sem.at[s]).start(priority=r % 2)

    def wait_rows(s):
        pltpu.make_async_copy(y_hbm.at[pl.ds(0, n_rows * SUBLANES)], ybuf.at[s],
                              row_sem.at[s]).wait()

    @pl.when(i == 0)
    def _():
        first = idx_copy(0, 0)
        first.start()
        first.wait()
        start_rows(0)
        idx_copy(nxt, 1).start()

    wait_rows(slot)
    idx_copy(nxt, 1 - slot).wait()
    start_rows(1 - slot)
    idx_copy(nxt2, slot).start()

    route = route_ref[...]
    lane = lax.broadcasted_iota(jnp.int32, route.shape, 1)
    w1 = jnp.sum(jnp.where(lane == 2, route, 0.0), axis=-1, keepdims=True)
    w2 = jnp.sum(jnp.where(lane == 3, route, 0.0), axis=-1, keepdims=True)
    y_ref = ybuf.at[slot]
    out = x_ref[...] + w1 * _load_token_tiles(y_ref, 0, tc) + w2 * _load_token_tiles(y_ref, tc, tc)
    o_ref[...] = _final_norm(out, gf_ref[...]) if final else out

    @pl.when(i == last)
    def _():
        wait_rows(1 - slot)
        idx_copy(nxt2, slot).wait()


def _combine(x, route, g_final, y_sorted, dest, final, tc):
    n, d = x.shape
    row = lambda width: pl.BlockSpec((tc, width), lambda i: (i, 0))
    return pl.pallas_call(
        functools.partial(_combine_kernel, final=final),
        out_shape=jax.ShapeDtypeStruct((n, d), F32),
        grid=(n // tc,),
        in_specs=[row(d), row(LANES), pl.BlockSpec((1, d), lambda i: (0, 0)),
                  pl.BlockSpec(memory_space=pl.ANY), pl.BlockSpec(memory_space=pl.ANY)],
        out_specs=row(d),
        scratch_shapes=[pltpu.VMEM((2, 2 * tc * SUBLANES, LANES), F32),
                        pltpu.SMEM((2, 2 * tc), jnp.int32),
                        pltpu.SemaphoreType.DMA((2,)), pltpu.SemaphoreType.DMA((2,))],
        compiler_params=_params("arbitrary"),
        name="expert_combine",
    )(x, route, g_final, y_sorted, dest)


def _routing_tables(route, tile_counts, n_experts, tm, tc):
    n = route.shape[0]
    n_router_tiles = tile_counts.shape[0]
    experts = jnp.arange(n_experts, dtype=jnp.int32)
    ids = route[:, :2].astype(jnp.int32).reshape(n_router_tiles, -1, 2)
    ranks = route[:, 4:6].astype(jnp.int32).reshape(n_router_tiles, -1, 2)
    counts = tile_counts.sum(axis=0)
    group = (counts + tm - 1) // tm * tm
    group_end = jnp.cumsum(group)
    tile_base = (group_end - group)[None, :] + jnp.cumsum(tile_counts, axis=0) - tile_counts
    chosen = ids[..., None] == experts
    dest = (jnp.sum(jnp.where(chosen, tile_base[:, None, None, :], 0), axis=-1) + ranks).reshape(n, 2)
    n_tiles = (2 * n) // tm + n_experts
    tile_start = jnp.arange(n_tiles, dtype=jnp.int32) * tm
    tile_expert = jnp.minimum(jnp.sum(tile_start[:, None] >= group_end[None, :], axis=1),
                              n_experts - 1).astype(jnp.int32)
    tile_expert = jnp.concatenate([tile_expert, (group_end[-1:] // tm).astype(jnp.int32)])
    dest_tiles = dest.reshape(n // tc, tc, 2).transpose(0, 2, 1).reshape(n // tc, 2 * tc)
    n_rows = n_tiles * tm
    pad_windows = jnp.minimum(group_end - group + counts, n_rows - tm)
    tail_windows = n_rows - tm * jnp.arange(1, n_experts + 1, dtype=jnp.int32)
    fill_start = jnp.concatenate([pad_windows, tail_windows]).astype(jnp.int32)
    return n_rows, tile_expert, dest_tiles, fill_start


def _pick_tile(n, want):
    tm = min(n, want)
    assert n % tm == 0 and tm % BLK == 0, (n, tm)
    return tm


def kernel(x, w_in, w_out, g_mix, g_ffn, g_sgu, sgu_w, sgu_b, g_out, ffn_w_gate, ffn_w_up,
           ffn_w_down, router_w, moe_w_gate, moe_w_up, moe_w_down, g_final):
    batch, seq, d = x.shape
    n = batch * seq
    depth = w_in.shape[0]
    n_groups = sgu_w.shape[1]
    d_sgu = n_groups * HEAD_DIM
    d_in = w_in.shape[2]
    d_sb = (d_in - 2 * d_sgu) // 3
    n_experts = router_w.shape[2]
    assert seq % BLK == 0 and sgu_w.shape[2] == BLK and d_sb % LANES == 0 and d_sgu % LANES == 0
    assert 3 * d_sb % d_sgu == 0 and d % LANES == 0

    col_scale = jnp.where(jnp.arange(d_in) < d_sb, HEAD_DIM ** -0.5, 1.0).astype(F32)[None, :]
    row2 = lambda a: a.reshape(1, -1).astype(F32)
    g_fin = row2(g_final)

    xf = x.reshape(n, d)
    for l in range(depth):
        last = l == depth - 1
        bias = jnp.repeat(sgu_b[l].T, HEAD_DIM, axis=1)
        qkv, y_sgu = _in_proj_sgu(xf, row2(g_mix[l]), col_scale, w_in[l].astype(BF16), sgu_w[l],
                                  bias, row2(g_sgu[l]), row2(g_out[l, d_sb:]), 3 * d_sb, d_sgu,
                                  _pick_tile(n, 512))
        y_sb = _attention(qkv, row2(g_out[l, :d_sb]), batch, seq, d_sb)
        i = l // 2
        if l % 2 == 0:
            d_ff = ffn_w_gate.shape[2]
            assert d_ff % FF_CHUNK == 0
            nc = d_ff // FF_CHUNK
            wg = ffn_w_gate[i].astype(BF16).reshape(d, nc, FF_CHUNK).transpose(1, 0, 2)
            wu = ffn_w_up[i].astype(BF16).reshape(d, nc, FF_CHUNK).transpose(1, 0, 2)
            wd = ffn_w_down[i].astype(BF16).reshape(nc, FF_CHUNK, d)
            xf = _out_proj_ffn(y_sb, y_sgu, xf, w_out[l].astype(BF16), row2(g_ffn[l]), wg, wu, wd,
                               g_fin, last, _pick_tile(n, 512))
        else:
            rw = jnp.pad(router_w[i].astype(F32), ((0, 0), (0, LANES - n_experts)))
            xf, h2, route, counts = _out_proj_router(y_sb, y_sgu, xf, w_out[l].astype(BF16),
                                                     row2(g_ffn[l]), rw, n_experts,
                                                     _pick_tile(n, 512))
            tile_counts = counts[::SUBLANES, :n_experts].astype(jnp.int32)
            tm, tc = _pick_tile(2 * n, MOE_ROW_TILE), _pick_tile(n, MOE_TOKEN_TILE)
            n_rows, tile_expert, dest, fill_start = _routing_tables(route, tile_counts, n_experts,
                                                                    tm, tc)
            x_sorted = _dispatch(h2, dest, fill_start, n_rows, tm, tc)
            y_sorted = _grouped_ffn(x_sorted, tile_expert, moe_w_gate[i].astype(BF16),
                                    moe_w_up[i].astype(BF16), moe_w_down[i].astype(BF16), tm)
            xf = _combine(xf, route, g_fin, y_sorted, dest, last, tc)
    return xf.reshape(batch, seq, d)
```

```python
import functools

import jax
import jax.numpy as jnp
from jax import lax
from jax.experimental import pallas as pl
from jax.experimental.pallas import tpu as pltpu

HEAD_DIM = 64
BLK = 128
LANES = 128
SUBLANES = 8
EPS = 1e-6
LOG2E = 1.4426950408889634
UNDERFLOW_LOG = -104.0
FF_CHUNK = 256
MOE_ROW_TILE = 512
MOE_TOKEN_TILE = 512
VMEM_LIMIT = 56 * 1024 * 1024

F32 = jnp.float32
BF16 = jnp.bfloat16


def _params(*sem):
    return pltpu.CompilerParams(dimension_semantics=sem, vmem_limit_bytes=VMEM_LIMIT)


def _split_bf16(a):
    hi = a.astype(BF16)
    lo = (a - hi.astype(F32)).astype(BF16)
    return hi, lo


def _same_head_mean_matrix():
    r = lax.broadcasted_iota(jnp.int32, (LANES, LANES), 0)
    c = lax.broadcasted_iota(jnp.int32, (LANES, LANES), 1)
    return jnp.where((r < HEAD_DIM) == (c < HEAD_DIM), 1.0 / HEAD_DIM, 0.0).astype(BF16)


def _head_rms_norm_many(ys, mean_mat, gain):
    mean2 = jnp.concatenate([mean_mat, mean_mat], axis=0)
    splits = [jnp.concatenate(_split_bf16(y * y), axis=1) for y in ys]
    means = [jnp.dot(s, mean2, preferred_element_type=F32) for s in splits]
    gains = gain if isinstance(gain, (list, tuple)) else [gain] * len(ys)
    return [y * lax.rsqrt(ms + EPS) * g for y, ms, g in zip(ys, means, gains)]


def _head_rms_norm(y, mean_mat, gain):
    return _head_rms_norm_many([y], mean_mat, gain)[0]


def _in_proj_sgu_kernel(x_ref, g_ref, cs_ref, w_ref, sw_ref, b_ref, gs_ref, go_ref, qkv_ref, y_ref):
    tm = x_ref.shape[0]
    d_qkv, d_sgu = qkv_ref.shape[1], y_ref.shape[1]
    x = x_ref[...]
    ms = jnp.mean(x * x, axis=-1, keepdims=True)
    h = (x * lax.rsqrt(ms + EPS) * g_ref[...]).astype(BF16)
    project = lambda lo, hi: jnp.dot(h, w_ref[:, lo:hi], preferred_element_type=F32)

    def qkv_part(c):
        cols = slice(c * d_sgu, (c + 1) * d_sgu)
        qkv_ref[:, cols] = (project(cols.start, cols.stop) * cs_ref[:, cols]).astype(qkv_ref.dtype)

    row = lax.broadcasted_iota(jnp.int32, (BLK, BLK), 0)
    col = lax.broadcasted_iota(jnp.int32, (BLK, BLK), 1)
    tril = col <= row
    head0 = col < HEAD_DIM
    mean_mat = _same_head_mean_matrix()
    blocks = [(slice(c * BLK, (c + 1) * BLK), slice(p * LANES, (p + 1) * LANES), p)
              for p in range(d_sgu // LANES) for c in range(tm // BLK)]
    mixers = [jnp.concatenate([jnp.where(tril, sw_ref[2 * p], 0.0),
                               jnp.where(tril, sw_ref[2 * p + 1], 0.0)], axis=0).astype(BF16)
              for p in range(d_sgu // LANES)]

    u = project(d_qkv, d_qkv + d_sgu)
    gate = project(d_qkv + d_sgu, d_qkv + 2 * d_sgu)
    gates = [jax.nn.gelu(gate[rs, ls]) for rs, ls, _ in blocks]
    gate_sq = [jnp.concatenate(_split_bf16(g * g), axis=1) for g in gates]
    qkv_part(0)
    mean2 = jnp.concatenate([mean_mat, mean_mat], axis=0)
    gates = [(g * lax.rsqrt(jnp.dot(s, mean2, preferred_element_type=F32) + EPS)
              * gs_ref[:, ls]).astype(BF16)
             for g, s, (_, ls, _) in zip(gates, gate_sq, blocks)]
    qkv_part(1)
    mixes = [jnp.dot(mixers[p], g, preferred_element_type=F32) for g, (_, _, p) in zip(gates, blocks)]
    ys = [jax.nn.gelu(u[rs, ls]) * (jnp.where(head0, m[:BLK], m[BLK:]) + b_ref[:, ls])
          for m, (rs, ls, _) in zip(mixes, blocks)]
    y_sq = [jnp.concatenate(_split_bf16(y * y), axis=1) for y in ys]
    qkv_part(2)
    for y, s, (rs, ls, _) in zip(ys, y_sq, blocks):
        norm = lax.rsqrt(jnp.dot(s, mean2, preferred_element_type=F32) + EPS)
        y_ref[rs, ls] = (y * norm * go_ref[:, ls]).astype(y_ref.dtype)


def _in_proj_sgu(x, g, col_scale, w, w_s, bias, g_gate, g_o, d_qkv, d_sgu, tm):
    n, d = x.shape
    assert d_qkv == 3 * d_sgu and w.shape[1] == d_qkv + 2 * d_sgu
    const = lambda a: pl.BlockSpec(a.shape, lambda i: (0,) * a.ndim)
    return pl.pallas_call(
        _in_proj_sgu_kernel,
        out_shape=[jax.ShapeDtypeStruct((n, d_qkv), BF16), jax.ShapeDtypeStruct((n, d_sgu), BF16)],
        grid=(n // tm,),
        in_specs=[
            pl.BlockSpec((tm, d), lambda i: (i, 0)),
            const(g), const(col_scale),
            pl.BlockSpec(w.shape, lambda i: (0, 0), pipeline_mode=pl.Buffered(1)),
            const(w_s), const(bias), const(g_gate), const(g_o),
        ],
        out_specs=[pl.BlockSpec((tm, d_qkv), lambda i: (i, 0)),
                   pl.BlockSpec((tm, d_sgu), lambda i: (i, 0))],
        compiler_params=_params("parallel"),
        name="in_proj_gating",
    )(x, g, col_scale, w, w_s, bias, g_gate, g_o)


def _attn_kernel(q_ref, k_ref, v_ref, g_ref, o_ref, q2_ref, v0_ref, v1_ref, acc_ref, z_ref, a_ref,
                 *, seq, pairs):
    nq = seq // BLK
    row = lax.broadcasted_iota(jnp.int32, (BLK, BLK), 0)
    col = lax.broadcasted_iota(jnp.int32, (BLK, BLK), 1)
    upper = jnp.where(row > col, 1.0, 0.0).astype(BF16)
    causal = jnp.concatenate([col < row, col < row], axis=0)
    mean_mat = _same_head_mean_matrix()
    lane = lax.broadcasted_iota(jnp.int32, (1, pairs * LANES), 1) % LANES
    m0 = jnp.where(lane < HEAD_DIM, 1.0, 0.0).astype(BF16)
    m1 = jnp.where(lane < HEAD_DIM, 0.0, 1.0).astype(BF16)

    v0_ref[...] = v_ref[...] * m0
    v1_ref[...] = v_ref[...] * m1
    for p in range(pairs):
        ls = slice(p * LANES, (p + 1) * LANES)
        q2_ref[p, :, :BLK, :] = (q_ref[:, ls] * m0[:, ls]).reshape(nq, BLK, LANES)
        q2_ref[p, :, BLK:, :] = (q_ref[:, ls] * m1[:, ls]).reshape(nq, BLK, LANES)

    lanes = [slice(p * LANES, (p + 1) * LANES) for p in range(pairs)]

    def scores(i, j, slot):
        ks = pl.multiple_of(j * BLK, BLK)
        for p in range(pairs):
            z_ref[slot, p] = lax.dot_general(q2_ref[p, i], k_ref[pl.ds(ks, BLK), lanes[p]],
                                             (((1,), (1,)), ((), ())),
                                             preferred_element_type=F32)

    def weighted_values(par, j):
        ks = pl.multiple_of(j * BLK, BLK)
        for p in range(pairs):
            acc_ref[par, p] += (
                jnp.dot(a_ref[par, p, :BLK], v0_ref[pl.ds(ks, BLK), lanes[p]],
                        preferred_element_type=F32)
                + jnp.dot(a_ref[par, p, BLK:], v1_ref[pl.ds(ks, BLK), lanes[p]],
                          preferred_element_type=F32))

    def weights(blocks, carries):
        staged = []
        for zs, mask in blocks:
            log_betas, sps = [], []
            for z in zs:
                e = jnp.exp2(jnp.abs(z) * (-LOG2E))
                sp = jnp.maximum(z, 0.0) + jnp.log(1.0 + e)
                log_betas.append(z - sp)
                sps.append(sp if mask is None else jnp.where(mask, sp, 0.0))
            staged.append((log_betas, sps))
        carries_in = []
        for _, sps in staged:
            carries_in.append(carries)
            rows = [jnp.sum(sp, axis=-1, keepdims=True) for sp in sps]
            carries = rows if carries is None else [c + r for c, r in zip(carries, rows)]
        low = functools.reduce(jnp.minimum, carries)
        alive = (jnp.min(low) < -UNDERFLOW_LOG).astype(jnp.int32)
        betweens = [[jnp.dot(sp.astype(BF16), upper, preferred_element_type=F32) for sp in sps]
                    for _, sps in staged]
        out = []
        for (log_betas, _), between, before, (_, mask) in zip(staged, betweens, carries_in, blocks):
            a_block = []
            for p in range(pairs):
                total = between[p] if before is None else between[p] + before[p]
                a = jnp.exp(log_betas[p] - total)
                if mask is not None:
                    a = jnp.where(mask, a, 0.0)
                a_block.append(a.astype(BF16))
            out.append(a_block)
        return alive, tuple(carries), out

    def z_block(slot):
        return [z_ref[slot, p] for p in range(pairs)]

    def keep_weights(par, a_block):
        for p in range(pairs):
            a_ref[par, p] = a_block[p]

    def finish(i, par, last_block):
        weighted_values(par, last_block)
        qs = pl.multiple_of(i * BLK, BLK)
        normed = _head_rms_norm_many([acc_ref[par, p] for p in range(pairs)], mean_mat,
                                     [g_ref[:, ls] for ls in lanes])
        for p in range(pairs):
            o_ref[pl.ds(qs, BLK), lanes[p]] = normed[p].astype(o_ref.dtype)

    def q_block(i, pending):
        par = i % 2
        for offset in range(3):
            scores(i, jnp.maximum(i - offset, 0), offset)
        finish(i - 1, 1 - par, pending)
        alive, carries, (a_diag, a_next) = weights([(z_block(0), causal), (z_block(1), None)], None)
        qs = pl.multiple_of(i * BLK, BLK)
        for p in range(pairs):
            acc_ref[par, p] = (
                jnp.dot(a_diag[p][:BLK], v0_ref[pl.ds(qs, BLK), lanes[p]], preferred_element_type=F32)
                + jnp.dot(a_diag[p][BLK:], v1_ref[pl.ds(qs, BLK), lanes[p]], preferred_element_type=F32))
        keep_weights(par, a_next)

        def more_keys(state):
            return jnp.logical_and(state[0] < i, state[1] > 0)

        def key_block(state):
            below, cs = state[0], state[2]
            zs = z_block((below + 1) % 3)
            weighted_values(par, i - below)
            scores(i, jnp.maximum(i - below - 2, 0), (below + 2) % 3)
            alive, cs, (a_block,) = weights([(zs, None)], cs)
            keep_weights(par, a_block)
            return below + 1, alive, cs

        done = lax.while_loop(more_keys, key_block, (jnp.int32(1), alive, carries))[0]
        return i - done

    acc_ref[0] = jnp.zeros(acc_ref.shape[1:], F32)
    scores(0, 0, 0)
    keep_weights(0, weights([(z_block(0), causal)], None)[2][0])
    pending = lax.fori_loop(1, nq, q_block, jnp.int32(0))
    finish(nq - 1, (nq - 1) % 2, pending)


def _attention(proj, g_sb, batch, seq, d_sb):
    n = proj.shape[0]
    pairs = d_sb // LANES
    blk = lambda off: pl.BlockSpec((seq, d_sb), lambda b: (b, off))
    return pl.pallas_call(
        functools.partial(_attn_kernel, seq=seq, pairs=pairs),
        out_shape=jax.ShapeDtypeStruct((n, d_sb), BF16),
        grid=(batch,),
        in_specs=[blk(0), blk(1), blk(2), pl.BlockSpec((1, d_sb), lambda b: (0, 0))],
        out_specs=pl.BlockSpec((seq, d_sb), lambda b: (b, 0)),
        scratch_shapes=[pltpu.VMEM((pairs, seq // BLK, 2 * BLK, LANES), BF16),
                        pltpu.VMEM((seq, d_sb), BF16),
                        pltpu.VMEM((seq, d_sb), BF16),
                        pltpu.VMEM((2, pairs, BLK, LANES), F32),
                        pltpu.VMEM((3, pairs, 2 * BLK, LANES), F32),
                        pltpu.VMEM((2, pairs, 2 * BLK, LANES), BF16)],
        compiler_params=_params("parallel"),
        name="stickbreak_attn",
    )(proj, proj, proj, g_sb)


def _top2_routes(logits_blocks, n_experts):
    shape = logits_blocks[0].shape
    assert shape == (BLK, LANES)
    lane = lax.broadcasted_iota(jnp.int32, shape, 1).astype(F32)
    neg = jnp.float32(-jnp.inf)
    first_lane = lambda hit: jnp.min(jnp.where(hit, lane, float(LANES)), axis=-1, keepdims=True)
    lgs = [jnp.where(lane < n_experts, lg, neg) for lg in logits_blocks]
    m1s = [jnp.max(lg, axis=-1, keepdims=True) for lg in lgs]
    i1s = [first_lane(lg == m1) for lg, m1 in zip(lgs, m1s)]
    lg2s = [jnp.where(lane == i1, neg, lg) for lg, i1 in zip(lgs, i1s)]
    m2s = [jnp.max(lg2, axis=-1, keepdims=True) for lg2 in lg2s]
    i2s = [first_lane(lg2 == m2) for lg2, m2 in zip(lg2s, m2s)]
    r = lax.broadcasted_iota(jnp.int32, shape, 0)
    c = lax.broadcasted_iota(jnp.int32, shape, 1)
    earlier = jnp.where(c < r, 1.0, 0.0).astype(BF16)
    ones = jnp.ones(shape, BF16)
    chosen = [jnp.where((lane == i1) | (lane == i2), 1.0, 0.0).astype(BF16)
              for i1, i2 in zip(i1s, i2s)]
    within = [jnp.dot(earlier, m, preferred_element_type=F32) for m in chosen]
    totals = [jnp.dot(ones, m, preferred_element_type=F32) for m in chosen]
    routes = []
    before = None
    for m1, m2, i1, i2, rank in zip(m1s, m2s, i1s, i2s, within):
        if before is not None:
            rank = rank + before
        e2 = jnp.exp(m2 - m1)
        w1 = 1.0 / (1.0 + e2)
        w2 = e2 / (1.0 + e2)
        r1 = jnp.sum(jnp.where(lane == i1, rank, 0.0), axis=-1, keepdims=True)
        r2 = jnp.sum(jnp.where(lane == i2, rank, 0.0), axis=-1, keepdims=True)
        routes.append(jnp.where(lane == 0.0, i1, 0.0) + jnp.where(lane == 1.0, i2, 0.0)
                      + jnp.where(lane == 2.0, w1, 0.0) + jnp.where(lane == 3.0, w2, 0.0)
                      + jnp.where(lane == 4.0, r1, 0.0) + jnp.where(lane == 5.0, r2, 0.0))
        total = totals[len(routes) - 1]
        before = total if before is None else before + total
    return routes, before


def _final_norm(x, gain):
    ms = jnp.mean(x * x, axis=-1, keepdims=True)
    return x * lax.rsqrt(ms + EPS) * gain


def _mix_residual(ysb_ref, ysgu_ref, x_ref, w_ref, g_ref):
    y = jnp.concatenate([ysb_ref[...], ysgu_ref[...]], axis=1)
    xn = x_ref[...] + jnp.dot(y, w_ref[...], preferred_element_type=F32)
    return xn, _final_norm(xn, g_ref[...])


def _out_proj_router_kernel(ysb_ref, ysgu_ref, x_ref, w_ref, g_ref, rw_ref,
                            xo_ref, h_ref, route_ref, cnt_ref, *, n_experts):
    xn, h = _mix_residual(ysb_ref, ysgu_ref, x_ref, w_ref, g_ref)
    xo_ref[...] = xn
    _store_token_tiles(h_ref, h)
    w_hi, w_lo = _split_bf16(rw_ref[...])
    w_both = jnp.concatenate([w_hi, w_lo], axis=1)
    blocks = [slice(r * BLK, (r + 1) * BLK) for r in range(h.shape[0] // BLK)]
    splits = [_split_bf16(h[rs]) for rs in blocks]
    firsts = [jnp.dot(hi, w_both, preferred_element_type=F32) for hi, _ in splits]
    seconds = [jnp.dot(lo, w_hi, preferred_element_type=F32) for _, lo in splits]
    routes, counts = _top2_routes(
        [a[:, :LANES] + a[:, LANES:] + b for a, b in zip(firsts, seconds)], n_experts)
    for rs, route in zip(blocks, routes):
        route_ref[rs, :] = route
    cnt_ref[...] = counts[:SUBLANES, :]


def _out_proj_router(ysb, ysgu, x, w, g, router_w, n_experts, tm):
    n, d = x.shape
    assert d == SUBLANES * LANES
    row_blk = lambda width: pl.BlockSpec((tm, width), lambda i: (i, 0))
    const = lambda a: pl.BlockSpec(a.shape, lambda i: (0, 0))
    return pl.pallas_call(
        functools.partial(_out_proj_router_kernel, n_experts=n_experts),
        out_shape=[jax.ShapeDtypeStruct((n, d), F32),
                   jax.ShapeDtypeStruct((n * SUBLANES, LANES), F32),
                   jax.ShapeDtypeStruct((n, LANES), F32),
                   jax.ShapeDtypeStruct((n // tm * SUBLANES, LANES), F32)],
        grid=(n // tm,),
        in_specs=[row_blk(ysb.shape[1]), row_blk(ysgu.shape[1]), row_blk(d), const(w), const(g),
                  const(router_w)],
        out_specs=[row_blk(d), pl.BlockSpec((tm * SUBLANES, LANES), lambda i: (i, 0)),
                   row_blk(LANES), pl.BlockSpec((SUBLANES, LANES), lambda i: (i, 0))],
        compiler_params=_params("parallel"),
        name="out_proj_router",
    )(ysb, ysgu, x, w, g, router_w)


def _out_proj_ffn_kernel(ysb_ref, ysgu_ref, x_ref, w_ref, g_ref, wg_ref, wu_ref, wd_ref, gf_ref,
                         o_ref, *, final):
    acc, h = _mix_residual(ysb_ref, ysgu_ref, x_ref, w_ref, g_ref)
    h = h.astype(BF16)
    for s, w in _ff_chunks(wg_ref.shape[1]):
        g = jnp.dot(h, wg_ref[:, s:s + w], preferred_element_type=F32)
        u = jnp.dot(h, wu_ref[:, s:s + w], preferred_element_type=F32)
        act = (jax.nn.silu(g) * u).astype(BF16)
        acc = acc + jnp.dot(act, wd_ref[s:s + w, :], preferred_element_type=F32)
    o_ref[...] = _final_norm(acc, gf_ref[...]) if final else acc


def _out_proj_ffn(ysb, ysgu, x, w, g, wg, wu, wd, g_final, final, tm):
    n, d = x.shape
    row_blk = lambda width: pl.BlockSpec((tm, width), lambda i: (i, 0))
    resident = lambda a: pl.BlockSpec(a.shape, lambda i: (0,) * a.ndim, pipeline_mode=pl.Buffered(1))
    return pl.pallas_call(
        functools.partial(_out_proj_ffn_kernel, final=final),
        out_shape=jax.ShapeDtypeStruct((n, d), F32),
        grid=(n // tm,),
        in_specs=[row_blk(ysb.shape[1]), row_blk(ysgu.shape[1]), row_blk(d), resident(w),
                  pl.BlockSpec((1, d), lambda i: (0, 0)),
                  resident(wg), resident(wu), resident(wd),
                  pl.BlockSpec((1, d), lambda i: (0, 0))],
        out_specs=row_blk(d),
        compiler_params=_params("parallel"),
        name="out_proj_dense_swiglu",
    )(ysb, ysgu, x, w, g, wg, wu, wd, g_final)


def _ff_chunks(d_ff):
    starts = list(range(0, d_ff, FF_CHUNK))
    return [(s, min(FF_CHUNK, d_ff - s)) for s in starts]


def _store_token_tiles(ref, value):
    rows = value.shape[0]
    for k in range(SUBLANES):
        ref[pl.ds(k, rows, stride=SUBLANES), :] = value[:, k * LANES:(k + 1) * LANES]


def _load_token_tiles(ref, first, rows):
    return jnp.concatenate([ref[pl.ds(first * SUBLANES + k, rows, stride=SUBLANES), :]
                            for k in range(SUBLANES)], axis=1)


def _dispatch_kernel(fill_ref, h_ref, idx_hbm, xs_hbm, idx_smem, zero_buf, row_sem, idx_sem,
                     fill_sem):
    i = pl.program_id(0)
    last = pl.num_programs(0) - 1
    slot = i % 2
    nxt = jnp.minimum(i + 1, last)
    tc = h_ref.shape[0] // SUBLANES

    def idx_copy(tile, s):
        return pltpu.make_async_copy(idx_hbm.at[tile], idx_smem.at[s], idx_sem.at[s])

    @pl.when(i == 0)
    def _():
        idx_copy(0, 0).start()
        zero_buf[...] = jnp.zeros_like(zero_buf)
        for j in range(fill_ref.shape[0]):
            first_row = pl.multiple_of(fill_ref[j] * SUBLANES, SUBLANES)
            fill = pltpu.make_async_copy(zero_buf, xs_hbm.at[pl.ds(first_row, zero_buf.shape[0])],
                                         fill_sem)
            fill.start()
            fill.wait()

    idx_copy(i, slot).wait()
    idx_copy(nxt, 1 - slot).start()
    for r in range(tc):
        for k in range(2):
            row = pl.multiple_of(idx_smem[slot, k * tc + r] * SUBLANES, SUBLANES)
            pltpu.make_async_copy(h_ref.at[pl.ds(r * SUBLANES, SUBLANES)],
                                  xs_hbm.at[pl.ds(row, SUBLANES)], row_sem).start(priority=k)
    for _ in range(2):
        pltpu.make_async_copy(h_ref, xs_hbm.at[pl.ds(0, tc * SUBLANES)], row_sem).wait()

    @pl.when(i == last)
    def _():
        idx_copy(nxt, 1 - slot).wait()


def _dispatch(h_tiles, dest, fill_start, n_rows, tm, tc):
    n_tok = h_tiles.shape[0] // SUBLANES
    grid_spec = pltpu.PrefetchScalarGridSpec(
        num_scalar_prefetch=1,
        grid=(n_tok // tc,),
        in_specs=[pl.BlockSpec((tc * SUBLANES, LANES), lambda i, fill: (i, 0)),
                  pl.BlockSpec(memory_space=pl.ANY)],
        out_specs=pl.BlockSpec(memory_space=pl.ANY),
        scratch_shapes=[pltpu.SMEM((2, 2 * tc), jnp.int32),
                        pltpu.VMEM((tm * SUBLANES, LANES), F32),
                        pltpu.SemaphoreType.DMA, pltpu.SemaphoreType.DMA((2,)),
                        pltpu.SemaphoreType.DMA],
    )
    return pl.pallas_call(
        _dispatch_kernel,
        out_shape=jax.ShapeDtypeStruct((n_rows * SUBLANES, LANES), F32),
        grid_spec=grid_spec,
        compiler_params=_params("arbitrary"),
        name="expert_dispatch",
    )(fill_start, h_tiles, dest)


def _grouped_ffn_kernel(te_ref, xs_ref, wg_ref, wu_ref, wd_ref, y_ref):
    i = pl.program_id(0)
    used = te_ref[pl.num_programs(0)]

    @pl.when(i < used)
    def _():
        tm = xs_ref.shape[0] // SUBLANES
        x = _load_token_tiles(xs_ref, 0, tm).astype(BF16)
        acc = None
        for s, w in _ff_chunks(wg_ref.shape[2]):
            g = jnp.dot(x, wg_ref[0, :, s:s + w], preferred_element_type=F32)
            u = jnp.dot(x, wu_ref[0, :, s:s + w], preferred_element_type=F32)
            act = (jax.nn.silu(g) * u).astype(BF16)
            part = jnp.dot(act, wd_ref[0, s:s + w, :], preferred_element_type=F32)
            acc = part if acc is None else acc + part
        _store_token_tiles(y_ref, acc)

    @pl.when(i >= used)
    def _():
        y_ref[...] = jnp.zeros_like(y_ref)


def _grouped_ffn(xs, tile_expert, wg, wu, wd, tm):
    n_tiles = tile_expert.shape[0] - 1
    _, d, d_ff = wg.shape
    rows = pl.BlockSpec((tm * SUBLANES, LANES), lambda i, te: (i, 0))
    grid_spec = pltpu.PrefetchScalarGridSpec(
        num_scalar_prefetch=1,
        grid=(n_tiles,),
        in_specs=[
            rows,
            pl.BlockSpec((1, d, d_ff), lambda i, te: (te[i], 0, 0)),
            pl.BlockSpec((1, d, d_ff), lambda i, te: (te[i], 0, 0)),
            pl.BlockSpec((1, d_ff, d), lambda i, te: (te[i], 0, 0)),
        ],
        out_specs=rows,
    )
    return pl.pallas_call(
        _grouped_ffn_kernel,
        out_shape=jax.ShapeDtypeStruct(xs.shape, F32),
        grid_spec=grid_spec,
        compiler_params=_params("arbitrary"),
        name="expert_swiglu",
    )(tile_expert, xs, wg, wu, wd)


def _combine_kernel(x_ref, route_ref, gf_ref, y_hbm, idx_hbm, o_ref,
                    ybuf, idx_smem, row_sem, idx_sem, *, final):
    tc = o_ref.shape[0]
    n_rows = 2 * tc
    i = pl.program_id(0)
    last = pl.num_programs(0) - 1
    slot = i % 2
    nxt = jnp.minimum(i + 1, last)
    nxt2 = jnp.minimum(i + 2, last)

    def idx_copy(tile, s):
        return pltpu.make_async_copy(idx_hbm.at[tile], idx_smem.at[s], idx_sem.at[s])

    def start_rows(s):
        for r in range(n_rows):
            row = pl.multiple_of(idx_smem[s, r] * SUBLANES, SUBLANES)
            pltpu.make_async_copy(y_hbm.at[pl.ds(row, SUBLANES)],
                                  ybuf.at[s, pl.ds(r * SUBLANES, SUBLANES)],
                                  row_sem.at[s]).start(priority=r % 2)

    def wait_rows(s):
        pltpu.make_async_copy(y_hbm.at[pl.ds(0, n_rows * SUBLANES)], ybuf.at[s],
                              row_sem.at[s]).wait()

    @pl.when(i == 0)
    def _():
        first = idx_copy(0, 0)
        first.start()
        first.wait()
        start_rows(0)
        idx_copy(nxt, 1).start()

    wait_rows(slot)
    idx_copy(nxt, 1 - slot).wait()
    start_rows(1 - slot)
    idx_copy(nxt2, slot).start()

    route = route_ref[...]
    lane = lax.broadcasted_iota(jnp.int32, route.shape, 1)
    w1 = jnp.sum(jnp.where(lane == 2, route, 0.0), axis=-1, keepdims=True)
    w2 = jnp.sum(jnp.where(lane == 3, route, 0.0), axis=-1, keepdims=True)
    y_ref = ybuf.at[slot]
    out = x_ref[...] + w1 * _load_token_tiles(y_ref, 0, tc) + w2 * _load_token_tiles(y_ref, tc, tc)
    o_ref[...] = _final_norm(out, gf_ref[...]) if final else out

    @pl.when(i == last)
    def _():
        wait_rows(1 - slot)
        idx_copy(nxt2, slot).wait()


def _combine(x, route, g_final, y_sorted, dest, final, tc):
    n, d = x.shape
    row = lambda width: pl.BlockSpec((tc, width), lambda i: (i, 0))
    return pl.pallas_call(
        functools.partial(_combine_kernel, final=final),
        out_shape=jax.ShapeDtypeStruct((n, d), F32),
        grid=(n // tc,),
        in_specs=[row(d), row(LANES), pl.BlockSpec((1, d), lambda i: (0, 0)),
                  pl.BlockSpec(memory_space=pl.ANY), pl.BlockSpec(memory_space=pl.ANY)],
        out_specs=row(d),
        scratch_shapes=[pltpu.VMEM((2, 2 * tc * SUBLANES, LANES), F32),
                        pltpu.SMEM((2, 2 * tc), jnp.int32),
                        pltpu.SemaphoreType.DMA((2,)), pltpu.SemaphoreType.DMA((2,))],
        compiler_params=_params("arbitrary"),
        name="expert_combine",
    )(x, route, g_final, y_sorted, dest)


def _routing_tables(route, tile_counts, n_experts, tm, tc):
    n = route.shape[0]
    n_router_tiles = tile_counts.shape[0]
    experts = jnp.arange(n_experts, dtype=jnp.int32)
    ids = route[:, :2].astype(jnp.int32).reshape(n_router_tiles, -1, 2)
    ranks = route[:, 4:6].astype(jnp.int32).reshape(n_router_tiles, -1, 2)
    counts = tile_counts.sum(axis=0)
    group = (counts + tm - 1) // tm * tm
    group_end = jnp.cumsum(group)
    tile_base = (group_end - group)[None, :] + jnp.cumsum(tile_counts, axis=0) - tile_counts
    chosen = ids[..., None] == experts
    dest = (jnp.sum(jnp.where(chosen, tile_base[:, None, None, :], 0), axis=-1) + ranks).reshape(n, 2)
    n_tiles = (2 * n) // tm + n_experts
    tile_start = jnp.arange(n_tiles, dtype=jnp.int32) * tm
    tile_expert = jnp.minimum(jnp.sum(tile_start[:, None] >= group_end[None, :], axis=1),
                              n_experts - 1).astype(jnp.int32)
    tile_expert = jnp.concatenate([tile_expert, (group_end[-1:] // tm).astype(jnp.int32)])
    dest_tiles = dest.reshape(n // tc, tc, 2).transpose(0, 2, 1).reshape(n // tc, 2 * tc)
    n_rows = n_tiles * tm
    pad_windows = jnp.minimum(group_end - group + counts, n_rows - tm)
    tail_windows = n_rows - tm * jnp.arange(1, n_experts + 1, dtype=jnp.int32)
    fill_start = jnp.concatenate([pad_windows, tail_windows]).astype(jnp.int32)
    return n_rows, tile_expert, dest_tiles, fill_start


def _pick_tile(n, want):
    tm = min(n, want)
    assert n % tm == 0 and tm % BLK == 0, (n, tm)
    return tm


def kernel(x, w_in, w_out, g_mix, g_ffn, g_sgu, sgu_w, sgu_b, g_out, ffn_w_gate, ffn_w_up,
           ffn_w_down, router_w, moe_w_gate, moe_w_up, moe_w_down, g_final):
    batch, seq, d = x.shape
    n = batch * seq
    depth = w_in.shape[0]
    n_groups = sgu_w.shape[1]
    d_sgu = n_groups * HEAD_DIM
    d_in = w_in.shape[2]
    d_sb = (d_in - 2 * d_sgu) // 3
    n_experts = router_w.shape[2]
    assert seq % BLK == 0 and sgu_w.shape[2] == BLK and d_sb % LANES == 0 and d_sgu % LANES == 0
    assert 3 * d_sb % d_sgu == 0 and d % LANES == 0

    col_scale = jnp.where(jnp.arange(d_in) < d_sb, HEAD_DIM ** -0.5, 1.0).astype(F32)[None, :]
    row2 = lambda a: a.reshape(1, -1).astype(F32)
    g_fin = row2(g_final)

    xf = x.reshape(n, d)
    for l in range(depth):
        last = l == depth - 1
        bias = jnp.repeat(sgu_b[l].T, HEAD_DIM, axis=1)
        qkv, y_sgu = _in_proj_sgu(xf, row2(g_mix[l]), col_scale, w_in[l].astype(BF16), sgu_w[l],
                                  bias, row2(g_sgu[l]), row2(g_out[l, d_sb:]), 3 * d_sb, d_sgu,
                                  _pick_tile(n, 512))
        y_sb = _attention(qkv, row2(g_out[l, :d_sb]), batch, seq, d_sb)
        i = l // 2
        if l % 2 == 0:
            xf = _out_proj_ffn(y_sb, y_sgu, xf, w_out[l].astype(BF16), row2(g_ffn[l]),
                               ffn_w_gate[i].astype(BF16), ffn_w_up[i].astype(BF16),
                               ffn_w_down[i].astype(BF16), g_fin, last, _pick_tile(n, 512))
        else:
            rw = jnp.pad(router_w[i].astype(F32), ((0, 0), (0, LANES - n_experts)))
            xf, h2, route, counts = _out_proj_router(y_sb, y_sgu, xf, w_out[l].astype(BF16),
                                                     row2(g_ffn[l]), rw, n_experts,
                                                     _pick_tile(n, 512))
            tile_counts = counts[::SUBLANES, :n_experts].astype(jnp.int32)
            tm, tc = _pick_tile(2 * n, MOE_ROW_TILE), _pick_tile(n, MOE_TOKEN_TILE)
            n_rows, tile_expert, dest, fill_start = _routing_tables(route, tile_counts, n_experts,
                                                                    tm, tc)
            x_sorted = _dispatch(h2, dest, fill_start, n_rows, tm, tc)
            y_sorted = _grouped_ffn(x_sorted, tile_expert, moe_w_gate[i].astype(BF16),
                                    moe_w_up[i].astype(BF16), moe_w_down[i].astype(BF16), tm)
            xf = _combine(xf, route, g_fin, y_sorted, dest, last, tc)
    return xf.reshape(batch, seq, d)
```

```python
import functools

import jax
import jax.numpy as jnp
from jax import lax
from jax.experimental import pallas as pl
from jax.experimental.pallas import tpu as pltpu

HEAD_DIM = 64
BLK = 128
LANES = 128
SUBLANES = 8
EPS = 1e-6
LOG2E = 1.4426950408889634
UNDERFLOW_LOG = -104.0
FF_CHUNK = 256
MOE_ROW_TILE = 512
MOE_TOKEN_TILE = 512
VMEM_LIMIT = 56 * 1024 * 1024

F32 = jnp.float32
BF16 = jnp.bfloat16


def _params(*sem):
    return pltpu.CompilerParams(dimension_semantics=sem, vmem_limit_bytes=VMEM_LIMIT)


def _split_bf16(a):
    hi = a.astype(BF16)
    lo = (a - hi.astype(F32)).astype(BF16)
    return hi, lo


def _same_head_mean_matrix():
    r = lax.broadcasted_iota(jnp.int32, (LANES, LANES), 0)
    c = lax.broadcasted_iota(jnp.int32, (LANES, LANES), 1)
    return jnp.where((r < HEAD_DIM) == (c < HEAD_DIM), 1.0 / HEAD_DIM, 0.0).astype(BF16)


def _head_rms_norm_many(ys, mean_mat, gain):
    squares = [(y * y).astype(BF16) for y in ys]
    means = [jnp.dot(s, mean_mat, preferred_element_type=F32) for s in squares]
    gains = gain if isinstance(gain, (list, tuple)) else [gain] * len(ys)
    return [y * lax.rsqrt(ms + EPS) * g for y, ms, g in zip(ys, means, gains)]


def _head_rms_norm(y, mean_mat, gain):
    return _head_rms_norm_many([y], mean_mat, gain)[0]


def _in_proj_sgu_kernel(x_ref, g_ref, cs_ref, w_ref, sw_ref, b_ref, gs_ref, go_ref, qkv_ref, y_ref):
    tm = x_ref.shape[0]
    d_qkv, d_sgu = qkv_ref.shape[1], y_ref.shape[1]
    x = x_ref[...]
    ms = jnp.mean(x * x, axis=-1, keepdims=True)
    h = (x * lax.rsqrt(ms + EPS) * g_ref[...]).astype(BF16)
    project = lambda lo, hi: jnp.dot(h, w_ref[:, lo:hi], preferred_element_type=F32)

    def qkv_part(c):
        cols = slice(c * d_sgu, (c + 1) * d_sgu)
        qkv_ref[:, cols] = (project(cols.start, cols.stop) * cs_ref[:, cols]).astype(qkv_ref.dtype)

    row = lax.broadcasted_iota(jnp.int32, (BLK, BLK), 0)
    col = lax.broadcasted_iota(jnp.int32, (BLK, BLK), 1)
    tril = col <= row
    head0 = col < HEAD_DIM
    mean_mat = _same_head_mean_matrix()
    blocks = [(slice(c * BLK, (c + 1) * BLK), slice(p * LANES, (p + 1) * LANES), p)
              for p in range(d_sgu // LANES) for c in range(tm // BLK)]
    mixers = [jnp.concatenate([jnp.where(tril, sw_ref[2 * p], 0.0),
                               jnp.where(tril, sw_ref[2 * p + 1], 0.0)], axis=0).astype(BF16)
              for p in range(d_sgu // LANES)]

    u = project(d_qkv, d_qkv + d_sgu)
    gate = project(d_qkv + d_sgu, d_qkv + 2 * d_sgu)
    gates = [jax.nn.gelu(gate[rs, ls]) for rs, ls, _ in blocks]
    gate_sq = [(g * g).astype(BF16) for g in gates]
    qkv_part(0)
    gates = [(g * lax.rsqrt(jnp.dot(s, mean_mat, preferred_element_type=F32) + EPS)
              * gs_ref[:, ls]).astype(BF16)
             for g, s, (_, ls, _) in zip(gates, gate_sq, blocks)]
    qkv_part(1)
    mixes = [jnp.dot(mixers[p], g, preferred_element_type=F32) for g, (_, _, p) in zip(gates, blocks)]
    ys = [jax.nn.gelu(u[rs, ls]) * (jnp.where(head0, m[:BLK], m[BLK:]) + b_ref[:, ls])
          for m, (rs, ls, _) in zip(mixes, blocks)]
    y_sq = [(y * y).astype(BF16) for y in ys]
    qkv_part(2)
    for y, s, (rs, ls, _) in zip(ys, y_sq, blocks):
        norm = lax.rsqrt(jnp.dot(s, mean_mat, preferred_element_type=F32) + EPS)
        y_ref[rs, ls] = (y * norm * go_ref[:, ls]).astype(y_ref.dtype)


def _in_proj_sgu(x, g, col_scale, w, w_s, bias, g_gate, g_o, d_qkv, d_sgu, tm):
    n, d = x.shape
    assert d_qkv == 3 * d_sgu and w.shape[1] == d_qkv + 2 * d_sgu
    const = lambda a: pl.BlockSpec(a.shape, lambda i: (0,) * a.ndim)
    return pl.pallas_call(
        _in_proj_sgu_kernel,
        out_shape=[jax.ShapeDtypeStruct((n, d_qkv), BF16), jax.ShapeDtypeStruct((n, d_sgu), BF16)],
        grid=(n // tm,),
        in_specs=[
            pl.BlockSpec((tm, d), lambda i: (i, 0)),
            const(g), const(col_scale),
            pl.BlockSpec(w.shape, lambda i: (0, 0), pipeline_mode=pl.Buffered(1)),
            const(w_s), const(bias), const(g_gate), const(g_o),
        ],
        out_specs=[pl.BlockSpec((tm, d_qkv), lambda i: (i, 0)),
                   pl.BlockSpec((tm, d_sgu), lambda i: (i, 0))],
        compiler_params=_params("parallel"),
        name="in_proj_gating",
    )(x, g, col_scale, w, w_s, bias, g_gate, g_o)


def _attn_kernel(q_ref, k_ref, v_ref, g_ref, o_ref, q2_ref, v0_ref, v1_ref, acc_ref, z_ref, a_ref,
                 *, seq, pairs):
    nq = seq // BLK
    row = lax.broadcasted_iota(jnp.int32, (BLK, BLK), 0)
    col = lax.broadcasted_iota(jnp.int32, (BLK, BLK), 1)
    upper = jnp.where(row > col, 1.0, 0.0).astype(BF16)
    causal = jnp.concatenate([col < row, col < row], axis=0)
    mean_mat = _same_head_mean_matrix()
    lane = lax.broadcasted_iota(jnp.int32, (1, pairs * LANES), 1) % LANES
    m0 = jnp.where(lane < HEAD_DIM, 1.0, 0.0).astype(BF16)
    m1 = jnp.where(lane < HEAD_DIM, 0.0, 1.0).astype(BF16)

    v0_ref[...] = v_ref[...] * m0
    v1_ref[...] = v_ref[...] * m1
    for p in range(pairs):
        ls = slice(p * LANES, (p + 1) * LANES)
        q2_ref[p, :, :BLK, :] = (q_ref[:, ls] * m0[:, ls]).reshape(nq, BLK, LANES)
        q2_ref[p, :, BLK:, :] = (q_ref[:, ls] * m1[:, ls]).reshape(nq, BLK, LANES)

    lanes = [slice(p * LANES, (p + 1) * LANES) for p in range(pairs)]

    def scores(i, j, slot):
        ks = pl.multiple_of(j * BLK, BLK)
        for p in range(pairs):
            z_ref[slot, p] = lax.dot_general(q2_ref[p, i], k_ref[pl.ds(ks, BLK), lanes[p]],
                                             (((1,), (1,)), ((), ())),
                                             preferred_element_type=F32)

    def weighted_values(par, j):
        ks = pl.multiple_of(j * BLK, BLK)
        for p in range(pairs):
            acc_ref[par, p] += (
                jnp.dot(a_ref[par, p, :BLK], v0_ref[pl.ds(ks, BLK), lanes[p]],
                        preferred_element_type=F32)
                + jnp.dot(a_ref[par, p, BLK:], v1_ref[pl.ds(ks, BLK), lanes[p]],
                          preferred_element_type=F32))

    def weights(blocks, carries):
        staged = []
        for zs, mask in blocks:
            log_betas, sps = [], []
            for z in zs:
                e = jnp.exp2(jnp.abs(z) * (-LOG2E))
                sp = jnp.maximum(z, 0.0) + jnp.log(1.0 + e)
                log_betas.append(z - sp)
                sps.append(sp if mask is None else jnp.where(mask, sp, 0.0))
            staged.append((log_betas, sps))
        carries_in = []
        for _, sps in staged:
            carries_in.append(carries)
            rows = [jnp.sum(sp, axis=-1, keepdims=True) for sp in sps]
            carries = rows if carries is None else [c + r for c, r in zip(carries, rows)]
        low = functools.reduce(jnp.minimum, carries)
        alive = (jnp.min(low) < -UNDERFLOW_LOG).astype(jnp.int32)
        betweens = [[jnp.dot(sp.astype(BF16), upper, preferred_element_type=F32) for sp in sps]
                    for _, sps in staged]
        out = []
        for (log_betas, _), between, before, (_, mask) in zip(staged, betweens, carries_in, blocks):
            a_block = []
            for p in range(pairs):
                total = between[p] if before is None else between[p] + before[p]
                a = jnp.exp(log_betas[p] - total)
                if mask is not None:
                    a = jnp.where(mask, a, 0.0)
                a_block.append(a.astype(BF16))
            out.append(a_block)
        return alive, tuple(carries), out

    def z_block(slot):
        return [z_ref[slot, p] for p in range(pairs)]

    def keep_weights(par, a_block):
        for p in range(pairs):
            a_ref[par, p] = a_block[p]

    def finish(i, par, last_block):
        weighted_values(par, last_block)
        qs = pl.multiple_of(i * BLK, BLK)
        normed = _head_rms_norm_many([acc_ref[par, p] for p in range(pairs)], mean_mat,
                                     [g_ref[:, ls] for ls in lanes])
        for p in range(pairs):
            o_ref[pl.ds(qs, BLK), lanes[p]] = normed[p].astype(o_ref.dtype)

    def q_block(i, pending):
        par = i % 2
        for offset in range(3):
            scores(i, jnp.maximum(i - offset, 0), offset)
        finish(i - 1, 1 - par, pending)
        alive, carries, (a_diag, a_next) = weights([(z_block(0), causal), (z_block(1), None)], None)
        qs = pl.multiple_of(i * BLK, BLK)
        for p in range(pairs):
            acc_ref[par, p] = (
                jnp.dot(a_diag[p][:BLK], v0_ref[pl.ds(qs, BLK), lanes[p]], preferred_element_type=F32)
                + jnp.dot(a_diag[p][BLK:], v1_ref[pl.ds(qs, BLK), lanes[p]], preferred_element_type=F32))
        keep_weights(par, a_next)

        def more_keys(state):
            return jnp.logical_and(state[0] < i, state[1] > 0)

        def key_block(state):
            below, cs = state[0], state[2]
            zs = z_block((below + 1) % 3)
            weighted_values(par, i - below)
            scores(i, jnp.maximum(i - below - 2, 0), (below + 2) % 3)
            alive, cs, (a_block,) = weights([(zs, None)], cs)
            keep_weights(par, a_block)
            return below + 1, alive, cs

        done = lax.while_loop(more_keys, key_block, (jnp.int32(1), alive, carries))[0]
        return i - done

    acc_ref[0] = jnp.zeros(acc_ref.shape[1:], F32)
    scores(0, 0, 0)
    keep_weights(0, weights([(z_block(0), causal)], None)[2][0])
    pending = lax.fori_loop(1, nq, q_block, jnp.int32(0))
    finish(nq - 1, (nq - 1) % 2, pending)


def _attention(proj, g_sb, batch, seq, d_sb):
    n = proj.shape[0]
    pairs = d_sb // LANES
    blk = lambda off: pl.BlockSpec((seq, d_sb), lambda b: (b, off))
    return pl.pallas_call(
        functools.partial(_attn_kernel, seq=seq, pairs=pairs),
        out_shape=jax.ShapeDtypeStruct((n, d_sb), BF16),
        grid=(batch,),
        in_specs=[blk(0), blk(1), blk(2), pl.BlockSpec((1, d_sb), lambda b: (0, 0))],
        out_specs=pl.BlockSpec((seq, d_sb), lambda b: (b, 0)),
        scratch_shapes=[pltpu.VMEM((pairs, seq // BLK, 2 * BLK, LANES), BF16),
                        pltpu.VMEM((seq, d_sb), BF16),
                        pltpu.VMEM((seq, d_sb), BF16),
                        pltpu.VMEM((2, pairs, BLK, LANES), F32),
                        pltpu.VMEM((3, pairs, 2 * BLK, LANES), F32),
                        pltpu.VMEM((2, pairs, 2 * BLK, LANES), BF16)],
        compiler_params=_params("parallel"),
        name="stickbreak_attn",
    )(proj, proj, proj, g_sb)


def _top2_routes(logits_blocks, n_experts):
    shape = logits_blocks[0].shape
    assert shape == (BLK, LANES)
    lane = lax.broadcasted_iota(jnp.int32, shape, 1).astype(F32)
    neg = jnp.float32(-jnp.inf)
    first_lane = lambda hit: jnp.min(jnp.where(hit, lane, float(LANES)), axis=-1, keepdims=True)
    lgs = [jnp.where(lane < n_experts, lg, neg) for lg in logits_blocks]
    m1s = [jnp.max(lg, axis=-1, keepdims=True) for lg in lgs]
    i1s = [first_lane(lg == m1) for lg, m1 in zip(lgs, m1s)]
    lg2s = [jnp.where(lane == i1, neg, lg) for lg, i1 in zip(lgs, i1s)]
    m2s = [jnp.max(lg2, axis=-1, keepdims=True) for lg2 in lg2s]
    i2s = [first_lane(lg2 == m2) for lg2, m2 in zip(lg2s, m2s)]
    r = lax.broadcasted_iota(jnp.int32, shape, 0)
    c = lax.broadcasted_iota(jnp.int32, shape, 1)
    earlier = jnp.where(c < r, 1.0, 0.0).astype(BF16)
    ones = jnp.ones(shape, BF16)
    chosen = [jnp.where((lane == i1) | (lane == i2), 1.0, 0.0).astype(BF16)
              for i1, i2 in zip(i1s, i2s)]
    within = [jnp.dot(earlier, m, preferred_element_type=F32) for m in chosen]
    totals = [jnp.dot(ones, m, preferred_element_type=F32) for m in chosen]
    routes = []
    before = None
    for m1, m2, i1, i2, rank in zip(m1s, m2s, i1s, i2s, within):
        if before is not None:
            rank = rank + before
        e2 = jnp.exp(m2 - m1)
        w1 = 1.0 / (1.0 + e2)
        w2 = e2 / (1.0 + e2)
        r1 = jnp.sum(jnp.where(lane == i1, rank, 0.0), axis=-1, keepdims=True)
        r2 = jnp.sum(jnp.where(lane == i2, rank, 0.0), axis=-1, keepdims=True)
        routes.append(jnp.where(lane == 0.0, i1, 0.0) + jnp.where(lane == 1.0, i2, 0.0)
                      + jnp.where(lane == 2.0, w1, 0.0) + jnp.where(lane == 3.0, w2, 0.0)
                      + jnp.where(lane == 4.0, r1, 0.0) + jnp.where(lane == 5.0, r2, 0.0))
        total = totals[len(routes) - 1]
        before = total if before is None else before + total
    return routes, before


def _final_norm(x, gain):
    ms = jnp.mean(x * x, axis=-1, keepdims=True)
    return x * lax.rsqrt(ms + EPS) * gain


def _mix_residual(ysb_ref, ysgu_ref, x_ref, w_ref, g_ref):
    y = jnp.concatenate([ysb_ref[...], ysgu_ref[...]], axis=1)
    xn = x_ref[...] + jnp.dot(y, w_ref[...], preferred_element_type=F32)
    return xn, _final_norm(xn, g_ref[...])


def _out_proj_router_kernel(ysb_ref, ysgu_ref, x_ref, w_ref, g_ref, rw_ref,
                            xo_ref, h_ref, route_ref, cnt_ref, *, n_experts):
    xn, h = _mix_residual(ysb_ref, ysgu_ref, x_ref, w_ref, g_ref)
    xo_ref[...] = xn
    _store_token_tiles(h_ref, h)
    w_hi, w_lo = _split_bf16(rw_ref[...])
    w_both = jnp.concatenate([w_hi, w_lo], axis=1)
    blocks = [slice(r * BLK, (r + 1) * BLK) for r in range(h.shape[0] // BLK)]
    splits = [_split_bf16(h[rs]) for rs in blocks]
    firsts = [jnp.dot(hi, w_both, preferred_element_type=F32) for hi, _ in splits]
    seconds = [jnp.dot(lo, w_hi, preferred_element_type=F32) for _, lo in splits]
    routes, counts = _top2_routes(
        [a[:, :LANES] + a[:, LANES:] + b for a, b in zip(firsts, seconds)], n_experts)
    for rs, route in zip(blocks, routes):
        route_ref[rs, :] = route
    cnt_ref[...] = counts[:SUBLANES, :]


def _out_proj_router(ysb, ysgu, x, w, g, router_w, n_experts, tm):
    n, d = x.shape
    assert d == SUBLANES * LANES
    row_blk = lambda width: pl.BlockSpec((tm, width), lambda i: (i, 0))
    const = lambda a: pl.BlockSpec(a.shape, lambda i: (0, 0))
    return pl.pallas_call(
        functools.partial(_out_proj_router_kernel, n_experts=n_experts),
        out_shape=[jax.ShapeDtypeStruct((n, d), F32),
                   jax.ShapeDtypeStruct((n * SUBLANES, LANES), F32),
                   jax.ShapeDtypeStruct((n, LANES), F32),
                   jax.ShapeDtypeStruct((n // tm * SUBLANES, LANES), F32)],
        grid=(n // tm,),
        in_specs=[row_blk(ysb.shape[1]), row_blk(ysgu.shape[1]), row_blk(d), const(w), const(g),
                  const(router_w)],
        out_specs=[row_blk(d), pl.BlockSpec((tm * SUBLANES, LANES), lambda i: (i, 0)),
                   row_blk(LANES), pl.BlockSpec((SUBLANES, LANES), lambda i: (i, 0))],
        compiler_params=_params("parallel"),
        name="out_proj_router",
    )(ysb, ysgu, x, w, g, router_w)


def _out_proj_ffn_kernel(ysb_ref, ysgu_ref, x_ref, w_ref, g_ref, wg_ref, wu_ref, wd_ref, gf_ref,
                         o_ref, *, final):
    acc, h = _mix_residual(ysb_ref, ysgu_ref, x_ref, w_ref, g_ref)
    h = h.astype(BF16)
    for s, w in _ff_chunks(wg_ref.shape[1]):
        g = jnp.dot(h, wg_ref[:, s:s + w], preferred_element_type=F32)
        u = jnp.dot(h, wu_ref[:, s:s + w], preferred_element_type=F32)
        act = (jax.nn.silu(g) * u).astype(BF16)
        acc = acc + jnp.dot(act, wd_ref[s:s + w, :], preferred_element_type=F32)
    o_ref[...] = _final_norm(acc, gf_ref[...]) if final else acc


def _out_proj_ffn(ysb, ysgu, x, w, g, wg, wu, wd, g_final, final, tm):
    n, d = x.shape
    row_blk = lambda width: pl.BlockSpec((tm, width), lambda i: (i, 0))
    resident = lambda a: pl.BlockSpec(a.shape, lambda i: (0,) * a.ndim, pipeline_mode=pl.Buffered(1))
    return pl.pallas_call(
        functools.partial(_out_proj_ffn_kernel, final=final),
        out_shape=jax.ShapeDtypeStruct((n, d), F32),
        grid=(n // tm,),
        in_specs=[row_blk(ysb.shape[1]), row_blk(ysgu.shape[1]), row_blk(d), resident(w),
                  pl.BlockSpec((1, d), lambda i: (0, 0)),
                  resident(wg), resident(wu), resident(wd),
                  pl.BlockSpec((1, d), lambda i: (0, 0))],
        out_specs=row_blk(d),
        compiler_params=_params("parallel"),
        name="out_proj_dense_swiglu",
    )(ysb, ysgu, x, w, g, wg, wu, wd, g_final)


def _ff_chunks(d_ff):
    starts = list(range(0, d_ff, FF_CHUNK))
    return [(s, min(FF_CHUNK, d_ff - s)) for s in starts]


def _store_token_tiles(ref, value):
    rows = value.shape[0]
    for k in range(SUBLANES):
        ref[pl.ds(k, rows, stride=SUBLANES), :] = value[:, k * LANES:(k + 1) * LANES]


def _load_token_tiles(ref, first, rows):
    return jnp.concatenate([ref[pl.ds(first * SUBLANES + k, rows, stride=SUBLANES), :]
                            for k in range(SUBLANES)], axis=1)


def _dispatch_kernel(fill_ref, h_ref, idx_hbm, xs_hbm, idx_smem, zero_buf, row_sem, idx_sem,
                     fill_sem):
    i = pl.program_id(0)
    last = pl.num_programs(0) - 1
    slot = i % 2
    nxt = jnp.minimum(i + 1, last)
    tc = h_ref.shape[0] // SUBLANES

    def idx_copy(tile, s):
        return pltpu.make_async_copy(idx_hbm.at[tile], idx_smem.at[s], idx_sem.at[s])

    @pl.when(i == 0)
    def _():
        idx_copy(0, 0).start()
        zero_buf[...] = jnp.zeros_like(zero_buf)
        for j in range(fill_ref.shape[0]):
            first_row = pl.multiple_of(fill_ref[j] * SUBLANES, SUBLANES)
            fill = pltpu.make_async_copy(zero_buf, xs_hbm.at[pl.ds(first_row, zero_buf.shape[0])],
                                         fill_sem)
            fill.start()
            fill.wait()

    idx_copy(i, slot).wait()
    idx_copy(nxt, 1 - slot).start()
    for r in range(tc):
        for k in range(2):
            row = pl.multiple_of(idx_smem[slot, k * tc + r] * SUBLANES, SUBLANES)
            pltpu.make_async_copy(h_ref.at[pl.ds(r * SUBLANES, SUBLANES)],
                                  xs_hbm.at[pl.ds(row, SUBLANES)], row_sem).start(priority=k)
    for _ in range(2):
        pltpu.make_async_copy(h_ref, xs_hbm.at[pl.ds(0, tc * SUBLANES)], row_sem).wait()

    @pl.when(i == last)
    def _():
        idx_copy(nxt, 1 - slot).wait()


def _dispatch(h_tiles, dest, fill_start, n_rows, tm, tc):
    n_tok = h_tiles.shape[0] // SUBLANES
    grid_spec = pltpu.PrefetchScalarGridSpec(
        num_scalar_prefetch=1,
        grid=(n_tok // tc,),
        in_specs=[pl.BlockSpec((tc * SUBLANES, LANES), lambda i, fill: (i, 0)),
                  pl.BlockSpec(memory_space=pl.ANY)],
        out_specs=pl.BlockSpec(memory_space=pl.ANY),
        scratch_shapes=[pltpu.SMEM((2, 2 * tc), jnp.int32),
                        pltpu.VMEM((tm * SUBLANES, LANES), F32),
                        pltpu.SemaphoreType.DMA, pltpu.SemaphoreType.DMA((2,)),
                        pltpu.SemaphoreType.DMA],
    )
    return pl.pallas_call(
        _dispatch_kernel,
        out_shape=jax.ShapeDtypeStruct((n_rows * SUBLANES, LANES), F32),
        grid_spec=grid_spec,
        compiler_params=_params("arbitrary"),
        name="expert_dispatch",
    )(fill_start, h_tiles, dest)


def _grouped_ffn_kernel(te_ref, xs_ref, wg_ref, wu_ref, wd_ref, y_ref):
    i = pl.program_id(0)
    used = te_ref[pl.num_programs(0)]

    @pl.when(i < used)
    def _():
        tm = xs_ref.shape[0] // SUBLANES
        x = _load_token_tiles(xs_ref, 0, tm).astype(BF16)
        acc = None
        for s, w in _ff_chunks(wg_ref.shape[2]):
            g = jnp.dot(x, wg_ref[0, :, s:s + w], preferred_element_type=F32)
            u = jnp.dot(x, wu_ref[0, :, s:s + w], preferred_element_type=F32)
            act = (jax.nn.silu(g) * u).astype(BF16)
            part = jnp.dot(act, wd_ref[0, s:s + w, :], preferred_element_type=F32)
            acc = part if acc is None else acc + part
        _store_token_tiles(y_ref, acc)

    @pl.when(i >= used)
    def _():
        y_ref[...] = jnp.zeros_like(y_ref)


def _grouped_ffn(xs, tile_expert, wg, wu, wd, tm):
    n_tiles = tile_expert.shape[0] - 1
    _, d, d_ff = wg.shape
    rows = pl.BlockSpec((tm * SUBLANES, LANES), lambda i, te: (i, 0))
    grid_spec = pltpu.PrefetchScalarGridSpec(
        num_scalar_prefetch=1,
        grid=(n_tiles,),
        in_specs=[
            rows,
            pl.BlockSpec((1, d, d_ff), lambda i, te: (te[i], 0, 0)),
            pl.BlockSpec((1, d, d_ff), lambda i, te: (te[i], 0, 0)),
            pl.BlockSpec((1, d_ff, d), lambda i, te: (te[i], 0, 0)),
        ],
        out_specs=rows,
    )
    return pl.pallas_call(
        _grouped_ffn_kernel,
        out_shape=jax.ShapeDtypeStruct(xs.shape, F32),
        grid_spec=grid_spec,
        compiler_params=_params("arbitrary"),
        name="expert_swiglu",
    )(tile_expert, xs, wg, wu, wd)


def _combine_kernel(x_ref, route_ref, gf_ref, y_hbm, idx_hbm, o_ref,
                    ybuf, idx_smem, row_sem, idx_sem, *, final):
    tc = o_ref.shape[0]
    n_rows = 2 * tc
    i = pl.program_id(0)
    last = pl.num_programs(0) - 1
    slot = i % 2
    nxt = jnp.minimum(i + 1, last)
    nxt2 = jnp.minimum(i + 2, last)

    def idx_copy(tile, s):
        return pltpu.make_async_copy(idx_hbm.at[tile], idx_smem.at[s], idx_sem.at[s])

    def start_rows(s):
        for r in range(n_rows):
            row = pl.multiple_of(idx_smem[s, r] * SUBLANES, SUBLANES)
            pltpu.make_async_copy(y_hbm.at[pl.ds(row, SUBLANES)],
                                  ybuf.at[s, pl.ds(r * SUBLANES, SUBLANES)],
                                  row_sem.at[s]).start(priority=r % 2)

    def wait_rows(s):
        pltpu.make_async_copy(y_hbm.at[pl.ds(0, n_rows * SUBLANES)], ybuf.at[s],
                              row_sem.at[s]).wait()

    @pl.when(i == 0)
    def _():
        first = idx_copy(0, 0)
        first.start()
        first.wait()
        start_rows(0)
        idx_copy(nxt, 1).start()

    wait_rows(slot)
    idx_copy(nxt, 1 - slot).wait()
    start_rows(1 - slot)
    idx_copy(nxt2, slot).start()

    route = route_ref[...]
    lane = lax.broadcasted_iota(jnp.int32, route.shape, 1)
    w1 = jnp.sum(jnp.where(lane == 2, route, 0.0), axis=-1, keepdims=True)
    w2 = jnp.sum(jnp.where(lane == 3, route, 0.0), axis=-1, keepdims=True)
    y_ref = ybuf.at[slot]
    out = x_ref[...] + w1 * _load_token_tiles(y_ref, 0, tc) + w2 * _load_token_tiles(y_ref, tc, tc)
    o_ref[...] = _final_norm(out, gf_ref[...]) if final else out

    @pl.when(i == last)
    def _():
        wait_rows(1 - slot)
        idx_copy(nxt2, slot).wait()


def _combine(x, route, g_final, y_sorted, dest, final, tc):
    n, d = x.shape
    row = lambda width: pl.BlockSpec((tc, width), lambda i: (i, 0))
    return pl.pallas_call(
        functools.partial(_combine_kernel, final=final),
        out_shape=jax.ShapeDtypeStruct((n, d), F32),
        grid=(n // tc,),
        in_specs=[row(d), row(LANES), pl.BlockSpec((1, d), lambda i: (0, 0)),
                  pl.BlockSpec(memory_space=pl.ANY), pl.BlockSpec(memory_space=pl.ANY)],
        out_specs=row(d),
        scratch_shapes=[pltpu.VMEM((2, 2 * tc * SUBLANES, LANES), F32),
                        pltpu.SMEM((2, 2 * tc), jnp.int32),
                        pltpu.SemaphoreType.DMA((2,)), pltpu.SemaphoreType.DMA((2,))],
        compiler_params=_params("arbitrary"),
        name="expert_combine",
    )(x, route, g_final, y_sorted, dest)


def _routing_tables(route, tile_counts, n_experts, tm, tc):
    n = route.shape[0]
    n_router_tiles = tile_counts.shape[0]
    experts = jnp.arange(n_experts, dtype=jnp.int32)
    ids = route[:, :2].astype(jnp.int32).reshape(n_router_tiles, -1, 2)
    ranks = route[:, 4:6].astype(jnp.int32).reshape(n_router_tiles, -1, 2)
    counts = tile_counts.sum(axis=0)
    group = (counts + tm - 1) // tm * tm
    group_end = jnp.cumsum(group)
    tile_base = (group_end - group)[None, :] + jnp.cumsum(tile_counts, axis=0) - tile_counts
    chosen = ids[..., None] == experts
    dest = (jnp.sum(jnp.where(chosen, tile_base[:, None, None, :], 0), axis=-1) + ranks).reshape(n, 2)
    n_tiles = (2 * n) // tm + n_experts
    tile_start = jnp.arange(n_tiles, dtype=jnp.int32) * tm
    tile_expert = jnp.minimum(jnp.sum(tile_start[:, None] >= group_end[None, :], axis=1),
                              n_experts - 1).astype(jnp.int32)
    tile_expert = jnp.concatenate([tile_expert, (group_end[-1:] // tm).astype(jnp.int32)])
    dest_tiles = dest.reshape(n // tc, tc, 2).transpose(0, 2, 1).reshape(n // tc, 2 * tc)
    n_rows = n_tiles * tm
    pad_windows = jnp.minimum(group_end - group + counts, n_rows - tm)
    tail_windows = n_rows - tm * jnp.arange(1, n_experts + 1, dtype=jnp.int32)
    fill_start = jnp.concatenate([pad_windows, tail_windows]).astype(jnp.int32)
    return n_rows, tile_expert, dest_tiles, fill_start


def _pick_tile(n, want):
    tm = min(n, want)
    assert n % tm == 0 and tm % BLK == 0, (n, tm)
    return tm


def kernel(x, w_in, w_out, g_mix, g_ffn, g_sgu, sgu_w, sgu_b, g_out, ffn_w_gate, ffn_w_up,
           ffn_w_down, router_w, moe_w_gate, moe_w_up, moe_w_down, g_final):
    batch, seq, d = x.shape
    n = batch * seq
    depth = w_in.shape[0]
    n_groups = sgu_w.shape[1]
    d_sgu = n_groups * HEAD_DIM
    d_in = w_in.shape[2]
    d_sb = (d_in - 2 * d_sgu) // 3
    n_experts = router_w.shape[2]
    assert seq % BLK == 0 and sgu_w.shape[2] == BLK and d_sb % LANES == 0 and d_sgu % LANES == 0
    assert 3 * d_sb % d_sgu == 0 and d % LANES == 0

    col_scale = jnp.where(jnp.arange(d_in) < d_sb, HEAD_DIM ** -0.5, 1.0).astype(F32)[None, :]
    row2 = lambda a: a.reshape(1, -1).astype(F32)
    g_fin = row2(g_final)

    xf = x.reshape(n, d)
    for l in range(depth):
        last = l == depth - 1
        bias = jnp.repeat(sgu_b[l].T, HEAD_DIM, axis=1)
        qkv, y_sgu = _in_proj_sgu(xf, row2(g_mix[l]), col_scale, w_in[l].astype(BF16), sgu_w[l],
                                  bias, row2(g_sgu[l]), row2(g_out[l, d_sb:]), 3 * d_sb, d_sgu,
                                  _pick_tile(n, 512))
        y_sb = _attention(qkv, row2(g_out[l, :d_sb]), batch, seq, d_sb)
        i = l // 2
        if l % 2 == 0:
            xf = _out_proj_ffn(y_sb, y_sgu, xf, w_out[l].astype(BF16), row2(g_ffn[l]),
                               ffn_w_gate[i].astype(BF16), ffn_w_up[i].astype(BF16),
                               ffn_w_down[i].astype(BF16), g_fin, last, _pick_tile(n, 512))
        else:
            rw = jnp.pad(router_w[i].astype(F32), ((0, 0), (0, LANES - n_experts)))
            xf, h2, route, counts = _out_proj_router(y_sb, y_sgu, xf, w_out[l].astype(BF16),
                                                     row2(g_ffn[l]), rw, n_experts,
                                                     _pick_tile(n, 512))
            tile_counts = counts[::SUBLANES, :n_experts].astype(jnp.int32)
            tm, tc = _pick_tile(2 * n, MOE_ROW_TILE), _pick_tile(n, MOE_TOKEN_TILE)
            n_rows, tile_expert, dest, fill_start = _routing_tables(route, tile_counts, n_experts,
                                                                    tm, tc)
            x_sorted = _dispatch(h2, dest, fill_start, n_rows, tm, tc)
            y_sorted = _grouped_ffn(x_sorted, tile_expert, moe_w_gate[i].astype(BF16),
                                    moe_w_up[i].astype(BF16), moe_w_down[i].astype(BF16), tm)
            xf = _combine(xf, route, g_fin, y_sorted, dest, last, tc)
    return xf.reshape(batch, seq, d)
```

```python
import functools

import jax
import jax.numpy as jnp
from jax import lax
from jax.experimental import pallas as pl
from jax.experimental.pallas import tpu as pltpu

HEAD_DIM = 64
BLK = 128
LANES = 128
SUBLANES = 8
EPS = 1e-6
LOG2E = 1.4426950408889634
UNDERFLOW_LOG = -104.0
FF_CHUNK = 256
TOKEN_TILE = 512
MOE_ROW_TILE = 512
MOE_TOKEN_TILE = 512
VMEM_LIMIT = 56 * 1024 * 1024

F32 = jnp.float32
BF16 = jnp.bfloat16


def _params(*sem):
    return pltpu.CompilerParams(dimension_semantics=sem, vmem_limit_bytes=VMEM_LIMIT)


def _split_bf16(a):
    hi = a.astype(BF16)
    lo = (a - hi.astype(F32)).astype(BF16)
    return hi, lo


def _same_head_mean_matrix():
    r = lax.broadcasted_iota(jnp.int32, (LANES, LANES), 0)
    c = lax.broadcasted_iota(jnp.int32, (LANES, LANES), 1)
    return jnp.where((r < HEAD_DIM) == (c < HEAD_DIM), 1.0 / HEAD_DIM, 0.0).astype(BF16)


def _head_rms_norm_many(ys, mean_mat, gain):
    squares = [(y * y).astype(BF16) for y in ys]
    means = [jnp.dot(s, mean_mat, preferred_element_type=F32) for s in squares]
    gains = gain if isinstance(gain, (list, tuple)) else [gain] * len(ys)
    return [y * lax.rsqrt(ms + EPS) * g for y, ms, g in zip(ys, means, gains)]


def _in_proj_sgu_kernel(x_ref, g_ref, cs_ref, w_ref, sw_ref, b_ref, gs_ref, go_ref, qkv_ref, y_ref):
    tm = x_ref.shape[0]
    d_qkv, d_sgu = qkv_ref.shape[1], y_ref.shape[1]
    x = x_ref[...]
    ms = jnp.mean(x * x, axis=-1, keepdims=True)
    h = (x * lax.rsqrt(ms + EPS) * g_ref[...]).astype(BF16)
    project = lambda lo, hi: jnp.dot(h, w_ref[:, lo:hi], preferred_element_type=F32)

    def qkv_part(c):
        cols = slice(c * d_sgu, (c + 1) * d_sgu)
        qkv_ref[:, cols] = (project(cols.start, cols.stop) * cs_ref[:, cols]).astype(qkv_ref.dtype)

    row = lax.broadcasted_iota(jnp.int32, (BLK, BLK), 0)
    col = lax.broadcasted_iota(jnp.int32, (BLK, BLK), 1)
    tril = col <= row
    head0 = col < HEAD_DIM
    mean_mat = _same_head_mean_matrix()
    blocks = [(slice(c * BLK, (c + 1) * BLK), slice(p * LANES, (p + 1) * LANES), p)
              for p in range(d_sgu // LANES) for c in range(tm // BLK)]
    mixers = [jnp.concatenate([jnp.where(tril, sw_ref[2 * p], 0.0),
                               jnp.where(tril, sw_ref[2 * p + 1], 0.0)], axis=0).astype(BF16)
              for p in range(d_sgu // LANES)]

    u = project(d_qkv, d_qkv + d_sgu)
    gate = project(d_qkv + d_sgu, d_qkv + 2 * d_sgu)
    gates = [jax.nn.gelu(gate[rs, ls]) for rs, ls, _ in blocks]
    gate_sq = [(g * g).astype(BF16) for g in gates]
    qkv_part(0)
    gates = [(g * lax.rsqrt(jnp.dot(s, mean_mat, preferred_element_type=F32) + EPS)
              * gs_ref[:, ls]).astype(BF16)
             for g, s, (_, ls, _) in zip(gates, gate_sq, blocks)]
    qkv_part(1)
    mixes = [jnp.dot(mixers[p], g, preferred_element_type=F32) for g, (_, _, p) in zip(gates, blocks)]
    ys = [jax.nn.gelu(u[rs, ls]) * (jnp.where(head0, m[:BLK], m[BLK:]) + b_ref[:, ls])
          for m, (rs, ls, _) in zip(mixes, blocks)]
    y_sq = [(y * y).astype(BF16) for y in ys]
    qkv_part(2)
    for y, s, (rs, ls, _) in zip(ys, y_sq, blocks):
        norm = lax.rsqrt(jnp.dot(s, mean_mat, preferred_element_type=F32) + EPS)
        y_ref[rs, ls] = (y * norm * go_ref[:, ls]).astype(y_ref.dtype)


def _in_proj_sgu(x, g, col_scale, w, w_s, bias, g_gate, g_o, d_qkv, d_sgu, tm):
    n, d = x.shape
    assert d_qkv == 3 * d_sgu and w.shape[1] == d_qkv + 2 * d_sgu
    const = lambda a: pl.BlockSpec(a.shape, lambda i: (0,) * a.ndim)
    return pl.pallas_call(
        _in_proj_sgu_kernel,
        out_shape=[jax.ShapeDtypeStruct((n, d_qkv), BF16), jax.ShapeDtypeStruct((n, d_sgu), BF16)],
        grid=(n // tm,),
        in_specs=[
            pl.BlockSpec((tm, d), lambda i: (i, 0)),
            const(g), const(col_scale),
            pl.BlockSpec(w.shape, lambda i: (0, 0), pipeline_mode=pl.Buffered(1)),
            const(w_s), const(bias), const(g_gate), const(g_o),
        ],
        out_specs=[pl.BlockSpec((tm, d_qkv), lambda i: (i, 0)),
                   pl.BlockSpec((tm, d_sgu), lambda i: (i, 0))],
        compiler_params=_params("parallel"),
        name="in_proj_gating",
    )(x, g, col_scale, w, w_s, bias, g_gate, g_o)


def _attn_kernel(q_ref, k_ref, v_ref, g_ref, o_ref, q2_ref, v0_ref, v1_ref, acc_ref, z_ref, a_ref,
                 *, seq, pairs):
    nq = seq // BLK
    row = lax.broadcasted_iota(jnp.int32, (BLK, BLK), 0)
    col = lax.broadcasted_iota(jnp.int32, (BLK, BLK), 1)
    upper = jnp.where(row > col, 1.0, 0.0).astype(BF16)
    causal = jnp.concatenate([col < row, col < row], axis=0)
    mean_mat = _same_head_mean_matrix()
    lane = lax.broadcasted_iota(jnp.int32, (1, pairs * LANES), 1) % LANES
    m0 = jnp.where(lane < HEAD_DIM, 1.0, 0.0).astype(BF16)
    m1 = jnp.where(lane < HEAD_DIM, 0.0, 1.0).astype(BF16)

    v0_ref[...] = v_ref[...] * m0
    v1_ref[...] = v_ref[...] * m1
    for p in range(pairs):
        ls = slice(p * LANES, (p + 1) * LANES)
        q2_ref[p, :, :BLK, :] = (q_ref[:, ls] * m0[:, ls]).reshape(nq, BLK, LANES)
        q2_ref[p, :, BLK:, :] = (q_ref[:, ls] * m1[:, ls]).reshape(nq, BLK, LANES)

    lanes = [slice(p * LANES, (p + 1) * LANES) for p in range(pairs)]

    def scores(i, j, slot):
        ks = pl.multiple_of(j * BLK, BLK)
        for p in range(pairs):
            z_ref[slot, p] = lax.dot_general(q2_ref[p, i], k_ref[pl.ds(ks, BLK), lanes[p]],
                                             (((1,), (1,)), ((), ())),
                                             preferred_element_type=F32)

    def weighted_values(par, j):
        ks = pl.multiple_of(j * BLK, BLK)
        for p in range(pairs):
            acc_ref[par, p] += (
                jnp.dot(a_ref[par, p, :BLK], v0_ref[pl.ds(ks, BLK), lanes[p]],
                        preferred_element_type=F32)
                + jnp.dot(a_ref[par, p, BLK:], v1_ref[pl.ds(ks, BLK), lanes[p]],
                          preferred_element_type=F32))

    def weights(blocks, carries):
        staged = []
        for zs, mask in blocks:
            log_betas, sps = [], []
            for z in zs:
                e = jnp.exp2(jnp.abs(z) * (-LOG2E))
                sp = jnp.maximum(z, 0.0) + jnp.log(1.0 + e)
                log_betas.append(z - sp)
                sps.append(sp if mask is None else jnp.where(mask, sp, 0.0))
            staged.append((log_betas, sps))
        carries_in = []
        for _, sps in staged:
            carries_in.append(carries)
            rows = [jnp.sum(sp, axis=-1, keepdims=True) for sp in sps]
            carries = rows if carries is None else [c + r for c, r in zip(carries, rows)]
        low = functools.reduce(jnp.minimum, carries)
        alive = (jnp.min(low) < -UNDERFLOW_LOG).astype(jnp.int32)
        betweens = [[jnp.dot(sp.astype(BF16), upper, preferred_element_type=F32) for sp in sps]
                    for _, sps in staged]
        out = []
        for (log_betas, _), between, before, (_, mask) in zip(staged, betweens, carries_in, blocks):
            a_block = []
            for p in range(pairs):
                total = between[p] if before is None else between[p] + before[p]
                a = jnp.exp(log_betas[p] - total)
                if mask is not None:
                    a = jnp.where(mask, a, 0.0)
                a_block.append(a.astype(BF16))
            out.append(a_block)
        return alive, tuple(carries), out

    def z_block(slot):
        return [z_ref[slot, p] for p in range(pairs)]

    def keep_weights(par, a_block):
        for p in range(pairs):
            a_ref[par, p] = a_block[p]

    def finish(i, par, last_block):
        weighted_values(par, last_block)
        qs = pl.multiple_of(i * BLK, BLK)
        normed = _head_rms_norm_many([acc_ref[par, p] for p in range(pairs)], mean_mat,
                                     [g_ref[:, ls] for ls in lanes])
        for p in range(pairs):
            o_ref[pl.ds(qs, BLK), lanes[p]] = normed[p].astype(o_ref.dtype)

    def q_block(i, pending):
        par = i % 2
        for offset in range(3):
            scores(i, jnp.maximum(i - offset, 0), offset)
        finish(i - 1, 1 - par, pending)
        alive, carries, (a_diag, a_next) = weights([(z_block(0), causal), (z_block(1), None)], None)
        qs = pl.multiple_of(i * BLK, BLK)
        for p in range(pairs):
            acc_ref[par, p] = (
                jnp.dot(a_diag[p][:BLK], v0_ref[pl.ds(qs, BLK), lanes[p]], preferred_element_type=F32)
                + jnp.dot(a_diag[p][BLK:], v1_ref[pl.ds(qs, BLK), lanes[p]], preferred_element_type=F32))
        keep_weights(par, a_next)

        def more_keys(state):
            return jnp.logical_and(state[0] < i, state[1] > 0)

        def key_block(state):
            below, cs = state[0], state[2]
            zs = z_block((below + 1) % 3)
            weighted_values(par, i - below)
            scores(i, jnp.maximum(i - below - 2, 0), (below + 2) % 3)
            alive, cs, (a_block,) = weights([(zs, None)], cs)
            keep_weights(par, a_block)
            return below + 1, alive, cs

        done = lax.while_loop(more_keys, key_block, (jnp.int32(1), alive, carries))[0]
        return i - done

    acc_ref[0] = jnp.zeros(acc_ref.shape[1:], F32)
    scores(0, 0, 0)
    keep_weights(0, weights([(z_block(0), causal)], None)[2][0])
    pending = lax.fori_loop(1, nq, q_block, jnp.int32(0))
    finish(nq - 1, (nq - 1) % 2, pending)


def _attention(proj, g_sb, batch, seq, d_sb):
    n = proj.shape[0]
    pairs = d_sb // LANES
    blk = lambda off: pl.BlockSpec((seq, d_sb), lambda b: (b, off))
    return pl.pallas_call(
        functools.partial(_attn_kernel, seq=seq, pairs=pairs),
        out_shape=jax.ShapeDtypeStruct((n, d_sb), BF16),
        grid=(batch,),
        in_specs=[blk(0), blk(1), blk(2), pl.BlockSpec((1, d_sb), lambda b: (0, 0))],
        out_specs=pl.BlockSpec((seq, d_sb), lambda b: (b, 0)),
        scratch_shapes=[pltpu.VMEM((pairs, seq // BLK, 2 * BLK, LANES), BF16),
                        pltpu.VMEM((seq, d_sb), BF16),
                        pltpu.VMEM((seq, d_sb), BF16),
                        pltpu.VMEM((2, pairs, BLK, LANES), F32),
                        pltpu.VMEM((3, pairs, 2 * BLK, LANES), F32),
                        pltpu.VMEM((2, pairs, 2 * BLK, LANES), BF16)],
        compiler_params=_params("parallel"),
        name="stickbreak_attn",
    )(proj, proj, proj, g_sb)


def _top2_routes(logits_blocks, n_experts):
    shape = logits_blocks[0].shape
    assert shape == (BLK, LANES)
    lane = lax.broadcasted_iota(jnp.int32, shape, 1).astype(F32)
    neg = jnp.float32(-jnp.inf)
    first_lane = lambda hit: jnp.min(jnp.where(hit, lane, float(LANES)), axis=-1, keepdims=True)
    lgs = [jnp.where(lane < n_experts, lg, neg) for lg in logits_blocks]
    m1s = [jnp.max(lg, axis=-1, keepdims=True) for lg in lgs]
    i1s = [first_lane(lg == m1) for lg, m1 in zip(lgs, m1s)]
    lg2s = [jnp.where(lane == i1, neg, lg) for lg, i1 in zip(lgs, i1s)]
    m2s = [jnp.max(lg2, axis=-1, keepdims=True) for lg2 in lg2s]
    i2s = [first_lane(lg2 == m2) for lg2, m2 in zip(lg2s, m2s)]
    r = lax.broadcasted_iota(jnp.int32, shape, 0)
    c = lax.broadcasted_iota(jnp.int32, shape, 1)
    earlier = jnp.where(c < r, 1.0, 0.0).astype(BF16)
    ones = jnp.ones(shape, BF16)
    chosen = [jnp.where((lane == i1) | (lane == i2), 1.0, 0.0).astype(BF16)
              for i1, i2 in zip(i1s, i2s)]
    within = [jnp.dot(earlier, m, preferred_element_type=F32) for m in chosen]
    totals = [jnp.dot(ones, m, preferred_element_type=F32) for m in chosen]
    routes = []
    before = None
    for m1, m2, i1, i2, rank in zip(m1s, m2s, i1s, i2s, within):
        if before is not None:
            rank = rank + before
        e2 = jnp.exp(m2 - m1)
        w1 = 1.0 / (1.0 + e2)
        w2 = e2 / (1.0 + e2)
        r1 = jnp.sum(jnp.where(lane == i1, rank, 0.0), axis=-1, keepdims=True)
        r2 = jnp.sum(jnp.where(lane == i2, rank, 0.0), axis=-1, keepdims=True)
        routes.append(jnp.where(lane == 0.0, i1, 0.0) + jnp.where(lane == 1.0, i2, 0.0)
                      + jnp.where(lane == 2.0, w1, 0.0) + jnp.where(lane == 3.0, w2, 0.0)
                      + jnp.where(lane == 4.0, r1, 0.0) + jnp.where(lane == 5.0, r2, 0.0))
        total = totals[len(routes) - 1]
        before = total if before is None else before + total
    return routes, before


def _final_norm(x, gain):
    ms = jnp.mean(x * x, axis=-1, keepdims=True)
    return x * lax.rsqrt(ms + EPS) * gain


def _mix_residual(ysb_ref, ysgu_ref, x_ref, w_ref, g_ref):
    y = jnp.concatenate([ysb_ref[...], ysgu_ref[...]], axis=1)
    xn = x_ref[...] + jnp.dot(y, w_ref[...], preferred_element_type=F32)
    return xn, _final_norm(xn, g_ref[...])


def _out_proj_router_kernel(ysb_ref, ysgu_ref, x_ref, w_ref, g_ref, rw_ref,
                            xo_ref, h_ref, route_ref, cnt_ref, *, n_experts):
    xn, h = _mix_residual(ysb_ref, ysgu_ref, x_ref, w_ref, g_ref)
    xo_ref[...] = xn
    _store_token_tiles(h_ref, h)
    w_hi, w_lo = _split_bf16(rw_ref[...])
    w_both = jnp.concatenate([w_hi, w_lo], axis=1)
    blocks = [slice(r * BLK, (r + 1) * BLK) for r in range(h.shape[0] // BLK)]
    splits = [_split_bf16(h[rs]) for rs in blocks]
    firsts = [jnp.dot(hi, w_both, preferred_element_type=F32) for hi, _ in splits]
    seconds = [jnp.dot(lo, w_hi, preferred_element_type=F32) for _, lo in splits]
    routes, counts = _top2_routes(
        [a[:, :LANES] + a[:, LANES:] + b for a, b in zip(firsts, seconds)], n_experts)
    for rs, route in zip(blocks, routes):
        route_ref[rs, :] = route
    cnt_ref[...] = counts[:SUBLANES, :]


def _out_proj_router(ysb, ysgu, x, w, g, router_w, n_experts, tm):
    n, d = x.shape
    assert d == SUBLANES * LANES
    row_blk = lambda width: pl.BlockSpec((tm, width), lambda i: (i, 0))
    const = lambda a: pl.BlockSpec(a.shape, lambda i: (0, 0))
    return pl.pallas_call(
        functools.partial(_out_proj_router_kernel, n_experts=n_experts),
        out_shape=[jax.ShapeDtypeStruct((n, d), F32),
                   jax.ShapeDtypeStruct((n * SUBLANES, LANES), F32),
                   jax.ShapeDtypeStruct((n, LANES), F32),
                   jax.ShapeDtypeStruct((n // tm * SUBLANES, LANES), F32)],
        grid=(n // tm,),
        in_specs=[row_blk(ysb.shape[1]), row_blk(ysgu.shape[1]), row_blk(d), const(w), const(g),
                  const(router_w)],
        out_specs=[row_blk(d), pl.BlockSpec((tm * SUBLANES, LANES), lambda i: (i, 0)),
                   row_blk(LANES), pl.BlockSpec((SUBLANES, LANES), lambda i: (i, 0))],
        compiler_params=_params("parallel"),
        name="out_proj_router",
    )(ysb, ysgu, x, w, g, router_w)


def _out_proj_ffn_kernel(ysb_ref, ysgu_ref, x_ref, w_ref, g_ref, wg_ref, wu_ref, wd_ref, gf_ref,
                         o_ref, *, final):
    acc, h = _mix_residual(ysb_ref, ysgu_ref, x_ref, w_ref, g_ref)
    h = h.astype(BF16)
    for s, w in _ff_chunks(wg_ref.shape[1]):
        g = jnp.dot(h, wg_ref[:, s:s + w], preferred_element_type=F32)
        u = jnp.dot(h, wu_ref[:, s:s + w], preferred_element_type=F32)
        act = (jax.nn.silu(g) * u).astype(BF16)
        acc = acc + jnp.dot(act, wd_ref[s:s + w, :], preferred_element_type=F32)
    o_ref[...] = _final_norm(acc, gf_ref[...]) if final else acc


def _out_proj_ffn(ysb, ysgu, x, w, g, wg, wu, wd, g_final, final, tm):
    n, d = x.shape
    row_blk = lambda width: pl.BlockSpec((tm, width), lambda i: (i, 0))
    resident = lambda a: pl.BlockSpec(a.shape, lambda i: (0,) * a.ndim, pipeline_mode=pl.Buffered(1))
    return pl.pallas_call(
        functools.partial(_out_proj_ffn_kernel, final=final),
        out_shape=jax.ShapeDtypeStruct((n, d), F32),
        grid=(n // tm,),
        in_specs=[row_blk(ysb.shape[1]), row_blk(ysgu.shape[1]), row_blk(d), resident(w),
                  pl.BlockSpec((1, d), lambda i: (0, 0)),
                  resident(wg), resident(wu), resident(wd),
                  pl.BlockSpec((1, d), lambda i: (0, 0))],
        out_specs=row_blk(d),
        compiler_params=_params("parallel"),
        name="out_proj_dense_swiglu",
    )(ysb, ysgu, x, w, g, wg, wu, wd, g_final)


def _ff_chunks(d_ff):
    starts = list(range(0, d_ff, FF_CHUNK))
    return [(s, min(FF_CHUNK, d_ff - s)) for s in starts]


def _store_token_tiles(ref, value):
    rows = value.shape[0]
    for k in range(SUBLANES):
        ref[pl.ds(k, rows, stride=SUBLANES), :] = value[:, k * LANES:(k + 1) * LANES]


def _load_token_tiles(ref, first, rows):
    return jnp.concatenate([ref[pl.ds(first * SUBLANES + k, rows, stride=SUBLANES), :]
                            for k in range(SUBLANES)], axis=1)


def _dispatch_kernel(fill_ref, h_ref, idx_hbm, xs_hbm, idx_smem, zero_buf, row_sem, idx_sem,
                     fill_sem):
    i = pl.program_id(0)
    last = pl.num_programs(0) - 1
    slot = i % 2
    nxt = jnp.minimum(i + 1, last)
    tc = h_ref.shape[0] // SUBLANES

    def idx_copy(tile, s):
        return pltpu.make_async_copy(idx_hbm.at[tile], idx_smem.at[s], idx_sem.at[s])

    @pl.when(i == 0)
    def _():
        idx_copy(0, 0).start()
        zero_buf[...] = jnp.zeros_like(zero_buf)
        for j in range(fill_ref.shape[0]):
            first_row = pl.multiple_of(fill_ref[j] * SUBLANES, SUBLANES)
            fill = pltpu.make_async_copy(zero_buf, xs_hbm.at[pl.ds(first_row, zero_buf.shape[0])],
                                         fill_sem)
            fill.start()
            fill.wait()

    idx_copy(i, slot).wait()
    idx_copy(nxt, 1 - slot).start()
    for r in range(tc):
        for k in range(2):
            row = pl.multiple_of(idx_smem[slot, k * tc + r] * SUBLANES, SUBLANES)
            pltpu.make_async_copy(h_ref.at[pl.ds(r * SUBLANES, SUBLANES)],
                                  xs_hbm.at[pl.ds(row, SUBLANES)], row_sem).start(priority=k)
    for _ in range(2):
        pltpu.make_async_copy(h_ref, xs_hbm.at[pl.ds(0, tc * SUBLANES)], row_sem).wait()

    @pl.when(i == last)
    def _():
        idx_copy(nxt, 1 - slot).wait()


def _dispatch(h_tiles, dest, fill_start, n_rows, tm, tc):
    n_tok = h_tiles.shape[0] // SUBLANES
    grid_spec = pltpu.PrefetchScalarGridSpec(
        num_scalar_prefetch=1,
        grid=(n_tok // tc,),
        in_specs=[pl.BlockSpec((tc * SUBLANES, LANES), lambda i, fill: (i, 0)),
                  pl.BlockSpec(memory_space=pl.ANY)],
        out_specs=pl.BlockSpec(memory_space=pl.ANY),
        scratch_shapes=[pltpu.SMEM((2, 2 * tc), jnp.int32),
                        pltpu.VMEM((tm * SUBLANES, LANES), F32),
                        pltpu.SemaphoreType.DMA, pltpu.SemaphoreType.DMA((2,)),
                        pltpu.SemaphoreType.DMA],
    )
    return pl.pallas_call(
        _dispatch_kernel,
        out_shape=jax.ShapeDtypeStruct((n_rows * SUBLANES, LANES), F32),
        grid_spec=grid_spec,
        compiler_params=_params("arbitrary"),
        name="expert_dispatch",
    )(fill_start, h_tiles, dest)


def _grouped_ffn_kernel(te_ref, xs_ref, wg_ref, wu_ref, wd_ref, y_ref):
    i = pl.program_id(0)
    used = te_ref[pl.num_programs(0)]

    @pl.when(i < used)
    def _():
        tm = xs_ref.shape[0] // SUBLANES
        x = _load_token_tiles(xs_ref, 0, tm).astype(BF16)
        acc = None
        for s, w in _ff_chunks(wg_ref.shape[2]):
            g = jnp.dot(x, wg_ref[0, :, s:s + w], preferred_element_type=F32)
            u = jnp.dot(x, wu_ref[0, :, s:s + w], preferred_element_type=F32)
            act = (jax.nn.silu(g) * u).astype(BF16)
            part = jnp.dot(act, wd_ref[0, s:s + w, :], preferred_element_type=F32)
            acc = part if acc is None else acc + part
        _store_token_tiles(y_ref, acc)

    @pl.when(i >= used)
    def _():
        y_ref[...] = jnp.zeros_like(y_ref)


def _grouped_ffn(xs, tile_expert, wg, wu, wd, tm):
    n_tiles = tile_expert.shape[0] - 1
    _, d, d_ff = wg.shape
    rows = pl.BlockSpec((tm * SUBLANES, LANES), lambda i, te: (i, 0))
    grid_spec = pltpu.PrefetchScalarGridSpec(
        num_scalar_prefetch=1,
        grid=(n_tiles,),
        in_specs=[
            rows,
            pl.BlockSpec((1, d, d_ff), lambda i, te: (te[i], 0, 0)),
            pl.BlockSpec((1, d, d_ff), lambda i, te: (te[i], 0, 0)),
            pl.BlockSpec((1, d_ff, d), lambda i, te: (te[i], 0, 0)),
        ],
        out_specs=rows,
    )
    return pl.pallas_call(
        _grouped_ffn_kernel,
        out_shape=jax.ShapeDtypeStruct(xs.shape, F32),
        grid_spec=grid_spec,
        compiler_params=_params("arbitrary"),
        name="expert_swiglu",
    )(tile_expert, xs, wg, wu, wd)


def _combine_kernel(x_ref, route_ref, gf_ref, y_hbm, idx_hbm, o_ref,
                    ybuf, idx_smem, row_sem, idx_sem, *, final):
    tc = o_ref.shape[0]
    n_rows = 2 * tc
    i = pl.program_id(0)
    last = pl.num_programs(0) - 1
    slot = i % 2
    nxt = jnp.minimum(i + 1, last)
    nxt2 = jnp.minimum(i + 2, last)

    def idx_copy(tile, s):
        return pltpu.make_async_copy(idx_hbm.at[tile], idx_smem.at[s], idx_sem.at[s])

    def start_rows(s):
        for r in range(n_rows):
            row = pl.multiple_of(idx_smem[s, r] * SUBLANES, SUBLANES)
            pltpu.make_async_copy(y_hbm.at[pl.ds(row, SUBLANES)],
                                  ybuf.at[s, pl.ds(r * SUBLANES, SUBLANES)],
                                  row_sem.at[s]).start(priority=r % 2)

    def wait_rows(s):
        pltpu.make_async_copy(y_hbm.at[pl.ds(0, n_rows * SUBLANES)], ybuf.at[s],
                              row_sem.at[s]).wait()

    @pl.when(i == 0)
    def _():
        first = idx_copy(0, 0)
        first.start()
        first.wait()
        start_rows(0)
        idx_copy(nxt, 1).start()

    wait_rows(slot)
    idx_copy(nxt, 1 - slot).wait()
    start_rows(1 - slot)
    idx_copy(nxt2, slot).start()

    route = route_ref[...]
    lane = lax.broadcasted_iota(jnp.int32, route.shape, 1)
    w1 = jnp.sum(jnp.where(lane == 2, route, 0.0), axis=-1, keepdims=True)
    w2 = jnp.sum(jnp.where(lane == 3, route, 0.0), axis=-1, keepdims=True)
    y_ref = ybuf.at[slot]
    out = x_ref[...] + w1 * _load_token_tiles(y_ref, 0, tc) + w2 * _load_token_tiles(y_ref, tc, tc)
    o_ref[...] = _final_norm(out, gf_ref[...]) if final else out

    @pl.when(i == last)
    def _():
        wait_rows(1 - slot)
        idx_copy(nxt2, slot).wait()


def _combine(x, route, g_final, y_sorted, dest, final, tc):
    n, d = x.shape
    row = lambda width: pl.BlockSpec((tc, width), lambda i: (i, 0))
    return pl.pallas_call(
        functools.partial(_combine_kernel, final=final),
        out_shape=jax.ShapeDtypeStruct((n, d), F32),
        grid=(n // tc,),
        in_specs=[row(d), row(LANES), pl.BlockSpec((1, d), lambda i: (0, 0)),
                  pl.BlockSpec(memory_space=pl.ANY), pl.BlockSpec(memory_space=pl.ANY)],
        out_specs=row(d),
        scratch_shapes=[pltpu.VMEM((2, 2 * tc * SUBLANES, LANES), F32),
                        pltpu.SMEM((2, 2 * tc), jnp.int32),
                        pltpu.SemaphoreType.DMA((2,)), pltpu.SemaphoreType.DMA((2,))],
        compiler_params=_params("arbitrary"),
        name="expert_combine",
    )(x, route, g_final, y_sorted, dest)


def _routing_tables(route, tile_counts, n_experts, tm, tc):
    n = route.shape[0]
    n_router_tiles = tile_counts.shape[0]
    experts = jnp.arange(n_experts, dtype=jnp.int32)
    ids = route[:, :2].astype(jnp.int32).reshape(n_router_tiles, -1, 2)
    ranks = route[:, 4:6].astype(jnp.int32).reshape(n_router_tiles, -1, 2)
    counts = tile_counts.sum(axis=0)
    group = (counts + tm - 1) // tm * tm
    group_end = jnp.cumsum(group)
    tile_base = (group_end - group)[None, :] + jnp.cumsum(tile_counts, axis=0) - tile_counts
    chosen = ids[..., None] == experts
    dest = (jnp.sum(jnp.where(chosen, tile_base[:, None, None, :], 0), axis=-1) + ranks).reshape(n, 2)
    n_tiles = (2 * n) // tm + n_experts
    tile_start = jnp.arange(n_tiles, dtype=jnp.int32) * tm
    tile_expert = jnp.minimum(jnp.sum(tile_start[:, None] >= group_end[None, :], axis=1),
                              n_experts - 1).astype(jnp.int32)
    tile_expert = jnp.concatenate([tile_expert, (group_end[-1:] // tm).astype(jnp.int32)])
    dest_tiles = dest.reshape(n // tc, tc, 2).transpose(0, 2, 1).reshape(n // tc, 2 * tc)
    n_rows = n_tiles * tm
    pad_windows = jnp.minimum(group_end - group + counts, n_rows - tm)
    tail_windows = n_rows - tm * jnp.arange(1, n_experts + 1, dtype=jnp.int32)
    fill_start = jnp.concatenate([pad_windows, tail_windows]).astype(jnp.int32)
    return n_rows, tile_expert, dest_tiles, fill_start


def _pick_tile(n, want):
    tm = min(n, want)
    assert n % tm == 0 and tm % BLK == 0, (n, tm)
    return tm


def kernel(x, w_in, w_out, g_mix, g_ffn, g_sgu, sgu_w, sgu_b, g_out, ffn_w_gate, ffn_w_up,
           ffn_w_down, router_w, moe_w_gate, moe_w_up, moe_w_down, g_final):
    batch, seq, d = x.shape
    n = batch * seq
    depth = w_in.shape[0]
    n_groups = sgu_w.shape[1]
    d_sgu = n_groups * HEAD_DIM
    d_in = w_in.shape[2]
    d_sb = (d_in - 2 * d_sgu) // 3
    n_experts = router_w.shape[2]
    assert seq % BLK == 0 and sgu_w.shape[2] == BLK and d_sb % LANES == 0 and d_sgu % LANES == 0
    assert 3 * d_sb % d_sgu == 0 and d % LANES == 0

    col_scale = jnp.where(jnp.arange(d_in) < d_sb, HEAD_DIM ** -0.5, 1.0).astype(F32)[None, :]
    row2 = lambda a: a.reshape(1, -1).astype(F32)
    g_fin = row2(g_final)

    xf = x.reshape(n, d)
    for l in range(depth):
        last = l == depth - 1
        bias = jnp.repeat(sgu_b[l].T, HEAD_DIM, axis=1)
        qkv, y_sgu = _in_proj_sgu(xf, row2(g_mix[l]), col_scale, w_in[l].astype(BF16), sgu_w[l],
                                  bias, row2(g_sgu[l]), row2(g_out[l, d_sb:]), 3 * d_sb, d_sgu,
                                  _pick_tile(n, TOKEN_TILE))
        y_sb = _attention(qkv, row2(g_out[l, :d_sb]), batch, seq, d_sb)
        i = l // 2
        if l % 2 == 0:
            xf = _out_proj_ffn(y_sb, y_sgu, xf, w_out[l].astype(BF16), row2(g_ffn[l]),
                               ffn_w_gate[i].astype(BF16), ffn_w_up[i].astype(BF16),
                               ffn_w_down[i].astype(BF16), g_fin, last, _pick_tile(n, TOKEN_TILE))
        else:
            rw = jnp.pad(router_w[i].astype(F32), ((0, 0), (0, LANES - n_experts)))
            xf, h2, route, counts = _out_proj_router(y_sb, y_sgu, xf, w_out[l].astype(BF16),
                                                     row2(g_ffn[l]), rw, n_experts,
                                                     _pick_tile(n, TOKEN_TILE))
            tile_counts = counts[::SUBLANES, :n_experts].astype(jnp.int32)
            tm, tc = _pick_tile(2 * n, MOE_ROW_TILE), _pick_tile(n, MOE_TOKEN_TILE)
            n_rows, tile_expert, dest, fill_start = _routing_tables(route, tile_counts, n_experts,
                                                                    tm, tc)
            x_sorted = _dispatch(h2, dest, fill_start, n_rows, tm, tc)
            y_sorted = _grouped_ffn(x_sorted, tile_expert, moe_w_gate[i].astype(BF16),
                                    moe_w_up[i].astype(BF16), moe_w_down[i].astype(BF16), tm)
            xf = _combine(xf, route, g_fin, y_sorted, dest, last, tc)
    return xf.reshape(batch, seq, d)
```
